```python
import math
import jax
import jax.numpy as jnp
from jax import lax
import numpy as np

D_MODEL = 1024
BATCH = 32
SEQ = 256
DEPTH = 2
DEC_BATCH = 8
DEC_SEQ = 4096
PAST_LEN = 256

GRID_W = 64
CHUNK = 64
EPS = 1e-6
HEAD_DIM = 64
D_MIX = D_MODEL
HG_W = D_MIX // 4
SSD_W = D_MIX // 2
RET_W = D_MIX - HG_W - SSD_W
HG_HEADS = HG_W // HEAD_DIM
HG_DK = HEAD_DIM
HG_DV = HEAD_DIM
SSD_HEADS = SSD_W // HEAD_DIM
SSD_P = HEAD_DIM
SSD_N = 64
SSD_GROUPS = 2
SSD_CONV = 3
SSD_CONV_CH = SSD_W + 2 * SSD_GROUPS * SSD_N
RET_HEADS = RET_W // HEAD_DIM
RET_DK = HEAD_DIM
RET_DV = HEAD_DIM
RET_BWD_OFFSET = 0.5
ROPE_BASE = 10000.0
HG_COLS = 5 * HG_W
SSD_COLS = SSD_W + SSD_CONV_CH + 2 * SSD_HEADS
RET_COLS = 4 * RET_W
D_IN = HG_COLS + SSD_COLS + RET_COLS
N_EXPERTS = 16
EC_CAPACITY_FACTOR = 2
D_EXPERT = 512

kernel_name = "hybrid_diffusion_ctx_prefix_step"


def rms_norm(x, g):
    xf = x.astype(jnp.float32)
    y = xf * lax.rsqrt(jnp.mean(xf * xf, axis=-1, keepdims=True) + EPS) * g.astype(jnp.float32)
    return y.astype(x.dtype)


def head_norm(o, n_heads, gain, centre):
    B, T, W = o.shape
    oh = o.astype(jnp.float32).reshape(B, T, n_heads, W // n_heads)
    if centre:
        oh = oh - jnp.mean(oh, axis=-1, keepdims=True)
    oh = oh * lax.rsqrt(jnp.mean(oh * oh, axis=-1, keepdims=True) + EPS)
    return oh.reshape(B, T, W) * gain.astype(jnp.float32)


def axial_rope(n_tokens):
    rows = n_tokens // GRID_W
    row = jnp.repeat(jnp.arange(rows, dtype=jnp.float32), GRID_W)
    col = jnp.tile(jnp.arange(GRID_W, dtype=jnp.float32), rows)
    n_freq = RET_DK // 4
    inv_freq = ROPE_BASE ** (-jnp.arange(n_freq, dtype=jnp.float32) / n_freq)
    ang = jnp.concatenate([row[:, None] * inv_freq, col[:, None] * inv_freq], axis=-1)
    return jnp.cos(ang), jnp.sin(ang)


def apply_rope(x, cos, sin):
    half = x.shape[-1] // 2
    x1, x2 = x[..., :half], x[..., half:]
    cos, sin = cos[None, :, None, :], sin[None, :, None, :]
    return jnp.concatenate([x1 * cos - x2 * sin, x1 * sin + x2 * cos], axis=-1)


def centred_depthwise_conv(u, w, b):
    ch = u.shape[-1]
    out = lax.conv_general_dilated(
        u, w[:, None, :].astype(u.dtype), window_strides=(1,),
        padding=[((SSD_CONV - 1) // 2, SSD_CONV // 2)],
        dimension_numbers=("NWC", "WIO", "NWC"), feature_group_count=ch)
    return out + b.astype(u.dtype)


def chunked_linear_recurrence(q, k, v, log_g, s0):
    B, T, H, K = q.shape
    V = v.shape[-1]
    Kd = log_g.shape[-1]
    n = T // CHUNK

    def to_chunks(a):
        a = a.astype(jnp.float32)
        return a.reshape(B, n, CHUNK, H, a.shape[-1]).transpose(1, 0, 3, 2, 4)

    causal = jnp.tril(jnp.ones((CHUNK, CHUNK), dtype=bool))[:, :, None]

    def step(S, blk):
        qb, kb, vb, gb = blk
        b = jnp.cumsum(gb, axis=2)
        diff = jnp.where(causal, b[:, :, :, None, :] - b[:, :, None, :, :], 0.0)
        dec = jnp.where(causal, jnp.exp(diff), 0.0)
        if Kd == 1:
            A = jnp.einsum("bhtk,bhsk->bhts", qb, kb) * dec[..., 0]
        else:
            A = jnp.einsum("bhtk,bhsk,bhtsk->bhts", qb, kb, dec)
        o = jnp.einsum("bhts,bhsv->bhtv", A, vb) + jnp.einsum("bhtk,bhkv->bhtv", qb * jnp.exp(b), S)
        b_end = b[:, :, -1:, :]
        S = jnp.exp(b_end[:, :, 0, :, None]) * S + jnp.einsum("bhsk,bhsv->bhkv", kb * jnp.exp(b_end - b), vb)
        return S, o

    S, o = lax.scan(step, s0.astype(jnp.float32), (to_chunks(q), to_chunks(k), to_chunks(v), to_chunks(log_g)))
    return o.transpose(1, 0, 3, 2, 4).reshape(B, T, H, V), S


def bidirectional_recurrence(fwd, bwd, s0):
    B, T, H, K = fwd[0].shape
    V = fwd[2].shape[-1]
    q, k, v, g = (jnp.concatenate([a, b[:, ::-1]], axis=2) for a, b in zip(fwd, bwd))
    o, s = chunked_linear_recurrence(q, k, v, g, s0.reshape(B, 2 * H, K, V))
    return o[:, :, :H] + o[:, ::-1, H:], s.reshape(B, 2, H, K, V)


def hgrn2_group(u, s0, lb, norm_g):
    B, T, _ = u.shape
    q, i, f_f, f_b, g = jnp.split(u, 5, axis=-1)
    heads = lambda a: a.reshape(B, T, HG_HEADS, -1)
    q = heads(jax.nn.silu(q.astype(jnp.float32))) * HG_DK ** -0.5
    v = heads(i.astype(jnp.float32))

    def direction(f_raw, lb_dir):
        fr = f_raw.astype(jnp.float32)
        f = lb_dir + (1.0 - lb_dir) * jax.nn.sigmoid(fr)
        key = (1.0 - lb_dir) * jax.nn.sigmoid(-fr)
        return (q, heads(key), v, heads(jnp.log(f)))

    o, s = bidirectional_recurrence(direction(f_f, lb[0]), direction(f_b, lb[1]), s0)
    o = head_norm(o.reshape(B, T, HG_W), HG_HEADS, norm_g, centre=False) * jax.nn.silu(g.astype(jnp.float32))
    return o.astype(u.dtype), s


def ssd_group(u, s0, conv_w, conv_b, dt_bias, a_log, d_skip, norm_g):
    B, T, _ = u.shape
    z, xbc, dt_f, dt_b = jnp.split(u, [SSD_W, SSD_W + SSD_CONV_CH, SSD_W + SSD_CONV_CH + SSD_HEADS], axis=-1)
    xbc = jax.nn.silu(centred_depthwise_conv(xbc, conv_w, conv_b)).astype(jnp.float32)
    xs, bm, cm = jnp.split(xbc, [SSD_W, SSD_W + SSD_GROUPS * SSD_N], axis=-1)
    xs = xs.reshape(B, T, SSD_HEADS, SSD_P)
    per_group = SSD_HEADS // SSD_GROUPS
    bm = jnp.repeat(bm.reshape(B, T, SSD_GROUPS, SSD_N), per_group, axis=2)
    cm = jnp.repeat(cm.reshape(B, T, SSD_GROUPS, SSD_N), per_group, axis=2)

    def direction(dt_raw, d):
        dt = jax.nn.softplus(dt_raw.astype(jnp.float32) + dt_bias[d].astype(jnp.float32))
        log_a = (dt * -jnp.exp(a_log[d].astype(jnp.float32)))[..., None]
        return (cm, bm, xs * dt[..., None], log_a)

    y, s = bidirectional_recurrence(direction(dt_f, 0), direction(dt_b, 1), s0)
    y = (y + xs * d_skip.astype(jnp.float32)[:, None]).reshape(B, T, SSD_W)
    y = rms_norm(y * jax.nn.silu(z.astype(jnp.float32)), norm_g)
    return y.astype(u.dtype), s


def retention_log_decay(offset):
    return jnp.log1p(-(2.0 ** (-5.0 - offset - jnp.arange(RET_HEADS, dtype=jnp.float32))))


def retention_group(u, s0, rope, norm_g):
    B, T, _ = u.shape
    q, k, v, g = jnp.split(u, 4, axis=-1)
    heads = lambda a: a.astype(jnp.float32).reshape(B, T, RET_HEADS, -1)
    q, k, v = heads(q), heads(k) * RET_DK ** -0.5, heads(v)
    if rope is not None:
        q, k = apply_rope(q, *rope), apply_rope(k, *rope)

    def log_decay(offset):
        return jnp.broadcast_to(retention_log_decay(offset)[:, None], (B, T, RET_HEADS, 1))

    o, s = bidirectional_recurrence((q, k, v, log_decay(0.0)), (q, k, v, log_decay(RET_BWD_OFFSET)), s0)
    o = head_norm(o.reshape(B, T, RET_W), RET_HEADS, norm_g, centre=True) * jax.nn.silu(g.astype(jnp.float32))
    return o.astype(u.dtype), s


def mixer_block(h, s0, rope, lb, l, p):
    proj = h @ p["w_in"][l]
    u_hg, u_ssd, u_ret = jnp.split(proj, [HG_COLS, HG_COLS + SSD_COLS], axis=-1)
    y_hg, st_hg = hgrn2_group(u_hg, s0[0], lb, p["hg_norm_g"][l])
    y_ssd, st_ssd = ssd_group(u_ssd, s0[1], p["ssd_conv_w"][l], p["ssd_conv_b"][l], p["ssd_dt_bias"][l],
                              p["ssd_a_log"][l], p["ssd_d"][l], p["ssd_norm_g"][l])
    y_ret, st_ret = retention_group(u_ret, s0[2], rope, p["ret_norm_g"][l])
    y = jnp.concatenate([y_hg, y_ssd, y_ret], axis=-1) @ p["w_out"][l]
    return y, (st_hg, st_ssd, st_ret)


def expert_choice_ffn(h, w_router, w_gate, w_up, w_down):
    B, N, D = h.shape
    cap = EC_CAPACITY_FACTOR * N // N_EXPERTS
    aff = jax.nn.softmax((h @ w_router).astype(jnp.float32), axis=-1)
    gate, idx = lax.top_k(jnp.swapaxes(aff, 1, 2), cap)
    xs = jax.vmap(lambda hb, ib: hb[ib])(h, idx)
    hid = jax.nn.silu(jnp.einsum("becd,edf->becf", xs, w_gate)) * jnp.einsum("becd,edf->becf", xs, w_up)
    ys = jnp.einsum("becf,efd->becd", hid, w_down) * gate[..., None].astype(h.dtype)
    return jax.vmap(lambda yb, ib: jnp.zeros((N, D), yb.dtype).at[ib.reshape(-1)].add(yb.reshape(-1, D)))(ys, idx)


def run_trunk(x, cond, s_hg, s_ssd, s_ret, rope, lb_all, p):
    B = x.shape[0]
    new_hg, new_ssd, new_ret = [], [], []
    for l in range(DEPTH):
        if s_hg is None:
            s0 = (jnp.zeros((B, 2, HG_HEADS, HG_DK, HG_DV), jnp.float32),
                  jnp.zeros((B, 2, SSD_HEADS, SSD_N, SSD_P), jnp.float32),
                  jnp.zeros((B, 2, RET_HEADS, RET_DK, RET_DV), jnp.float32))
        else:
            s0 = (s_hg[:, l], s_ssd[:, l], s_ret[:, l])
        mod = jax.nn.silu(cond) @ p["w_ada"][l] + p["b_ada"][l]
        sh1, sc1, g1, sh2, sc2, g2 = jnp.split(mod[:, None, :], 6, axis=-1)
        h = rms_norm(x, p["norm1_g"][l]) * (1 + sc1) + sh1
        y, (st_hg, st_ssd, st_ret) = mixer_block(h, s0, rope, lb_all[l], l, p)
        x = x + g1 * y
        h = rms_norm(x, p["norm2_g"][l]) * (1 + sc2) + sh2
        x = x + g2 * expert_choice_ffn(h, p["w_router"][l], p["w_gate"][l], p["w_up"][l], p["w_down"][l])
        new_hg.append(st_hg)
        new_ssd.append(st_ssd)
        new_ret.append(st_ret)
    return rms_norm(x, p["final_g"]), jnp.stack(new_hg, axis=1), jnp.stack(new_ssd, axis=1), jnp.stack(new_ret, axis=1)


def setup_inputs(seed: int = 0) -> dict:
    key = jax.random.key(seed)
    ks = iter(jax.random.split(key, 32))
    f32 = jnp.float32

    def nrm(shape, scale):
        return jax.random.normal(next(ks), shape, f32) * scale

    L, D = DEPTH, D_MODEL
    dt0 = jnp.exp(jax.random.uniform(next(ks), (L, 2, SSD_HEADS), f32, math.log(1e-3), math.log(1e-1)))
    a0 = jax.random.uniform(next(ks), (L, 2, SSD_HEADS), f32, 1.0, 16.0)
    return {
        "x_prompt": nrm((BATCH, SEQ, D), 1.0),
        "x_sample": nrm((DEC_BATCH, DEC_SEQ, D), 1.0),
        "state_hgrn": nrm((DEC_BATCH, L, 2, HG_HEADS, HG_DK, HG_DV), 0.5),
        "state_ssd": nrm((DEC_BATCH, L, 2, SSD_HEADS, SSD_N, SSD_P), 0.5),
        "state_ret": nrm((DEC_BATCH, L, 2, RET_HEADS, RET_DK, RET_DV), 0.5),
        "c": nrm((DEC_BATCH, D), 1.0),
        "c_ctx": nrm((D,), 1.0),
        "norm1_g": 1.0 + nrm((L, D), 0.02),
        "norm2_g": 1.0 + nrm((L, D), 0.02),
        "final_g": 1.0 + nrm((D,), 0.02),
        "w_ada": nrm((L, D, 6 * D), D ** -0.5),
        "b_ada": nrm((L, 6 * D), 0.02),
        "w_in": nrm((L, D, D_IN), D ** -0.5),
        "w_out": nrm((L, D_MIX, D), D_MIX ** -0.5),
        "hg_lb": nrm((L, 2, HG_W), 1.0),
        "hg_norm_g": 1.0 + nrm((L, HG_W), 0.02),
        "ssd_conv_w": nrm((L, SSD_CONV, SSD_CONV_CH), SSD_CONV ** -0.5),
        "ssd_conv_b": nrm((L, SSD_CONV_CH), 0.02),
        "ssd_dt_bias": dt0 + jnp.log(-jnp.expm1(-dt0)),
        "ssd_a_log": jnp.log(a0),
        "ssd_d": 1.0 + nrm((L, SSD_HEADS), 0.1),
        "ssd_norm_g": 1.0 + nrm((L, SSD_W), 0.02),
        "ret_norm_g": 1.0 + nrm((L, RET_W), 0.02),
        "w_router": nrm((L, D, N_EXPERTS), D ** -0.5),
        "w_gate": nrm((L, N_EXPERTS, D, D_EXPERT), D ** -0.5),
        "w_up": nrm((L, N_EXPERTS, D, D_EXPERT), D ** -0.5),
        "w_down": nrm((L, N_EXPERTS, D_EXPERT, D), D_EXPERT ** -0.5),
    }


def reference(x_prompt, x_sample, state_hgrn, state_ssd, state_ret, c, c_ctx, norm1_g, norm2_g, final_g,
              w_ada, b_ada, w_in, w_out, hg_lb, hg_norm_g, ssd_conv_w, ssd_conv_b, ssd_dt_bias, ssd_a_log,
              ssd_d, ssd_norm_g, ret_norm_g, w_router, w_gate, w_up, w_down):
    p = {
        "norm1_g": norm1_g, "norm2_g": norm2_g, "final_g": final_g, "w_ada": w_ada, "b_ada": b_ada,
        "w_in": w_in, "w_out": w_out, "hg_norm_g": hg_norm_g, "ssd_conv_w": ssd_conv_w,
        "ssd_conv_b": ssd_conv_b, "ssd_dt_bias": ssd_dt_bias, "ssd_a_log": ssd_a_log, "ssd_d": ssd_d,
        "ssd_norm_g": ssd_norm_g, "ret_norm_g": ret_norm_g, "w_router": w_router, "w_gate": w_gate,
        "w_up": w_up, "w_down": w_down,
    }
    lb_p = jax.nn.softmax(hg_lb.astype(jnp.float32), axis=0)
    lb_all = jnp.cumsum(lb_p, axis=0) - lb_p[0]
    y_prompt, new_state_hgrn, new_state_ssd, new_state_ret = run_trunk(
        x_prompt, c_ctx[None, :], None, None, None, None, lb_all, p)
    rope = axial_rope(x_sample.shape[1])
    y_sample, _, _, _ = run_trunk(x_sample, c, state_hgrn, state_ssd, state_ret, rope, lb_all, p)
    return (y_prompt, y_sample, new_state_hgrn, new_state_ssd, new_state_ret)
```

```python
import functools
import math

import jax
import jax.numpy as jnp
import numpy as np
from jax import lax
from jax.experimental import pallas as pl
from jax.experimental.pallas import tpu as pltpu

D_MODEL = 1024
DEPTH = 2
GRID_W = 64
CHUNK = 64
EPS = 1e-6
HEAD_DIM = 64
HG_W = 256
SSD_W = 512
RET_W = 256
HG_HEADS = 4
SSD_HEADS = 8
SSD_N = 64
SSD_GROUPS = 2
SSD_CONV_CH = 768
RET_HEADS = 4
RET_BWD_OFFSET = 0.5
ROPE_BASE = 10000.0
N_EXPERTS = 16
EC_CAPACITY_FACTOR = 2
D_EXPERT = 512

C_XBC = 0
C_Z = 768
C_DTF = 1280
C_DTB = 1792
C_HG = 2304
C_RET = 3584
P_COLS = 4608

TB = 256
NCH = TB // CHUNK
SUB = 8
NSUB = CHUNK // SUB
EXP_CLAMP = 80.0

F32 = jnp.float32
BF16 = jnp.bfloat16


def _sigmoid(x):
    return 1.0 / (1.0 + jnp.exp(-x))


def _silu(x):
    return x * _sigmoid(x)


def _softplus(x):
    return jnp.maximum(x, 0.0) + jnp.log1p(jnp.exp(-jnp.abs(x)))


def _dot(a, b):
    return jnp.dot(a, b, preferred_element_type=F32)


def _dot_nt(a, b):
    return lax.dot_general(a, b, (((1,), (1,)), ((), ())), preferred_element_type=F32)


def _dot_tn(a, b):
    return lax.dot_general(a, b, (((0,), (0,)), ((), ())), preferred_element_type=F32)


def _split3(x):
    hi = x.astype(BF16)
    r = x - hi.astype(F32)
    mid = r.astype(BF16)
    lo = (r - mid.astype(F32)).astype(BF16)
    return hi, mid, lo


def _seg_mean(x, ones_bd):
    hi = x.astype(BF16)
    lo = (x - hi.astype(F32)).astype(BF16)
    return (_dot(hi, ones_bd) + _dot(lo, ones_bd)) * (1.0 / HEAD_DIM)


def _tile_rows(x, n):
    return jnp.concatenate([x] * n, axis=0)


def _mixer_kernel(tbl_ref, proj_ref, prev_ref, next_ref, cos_ref, sin_ref, tri_ref,
                  convw_ref, convb_ref, dtb_ref, alog_ref, dskip_ref, lb_ref,
                  hgn_ref, ssdn_ref, retn_ref, retd_ref, reteq_ref, retek_ref, reteend_ref,
                  sssd_in_ref, sret_in_ref, shg_in_ref, *rest, reverse):
    if reverse:
        (y_ref, sssd_out_ref, sret_out_ref, shg_out_ref,
         xbc_scr, g_scr, b_scr, o_scr, s_ssd, s_ret, s_hg) = rest
        obwd_ref = None
    else:
        (obwd_ref, y_ref, sssd_out_ref, sret_out_ref, shg_out_ref,
         xbc_scr, g_scr, b_scr, o_scr, s_ssd, s_ret, s_hg) = rest

    step = pl.program_id(0)
    is_first = tbl_ref[step, 2]
    keep_prev = (1 - tbl_ref[step, 3]).astype(F32)
    keep_next = (1 - tbl_ref[step, 4]).astype(F32)

    r256 = lax.broadcasted_iota(jnp.int32, (256, 256), 0)
    c256 = lax.broadcasted_iota(jnp.int32, (256, 256), 1)
    bd = (r256 >> 6) == (c256 >> 6)
    gmask_s = (lax.broadcasted_iota(jnp.int32, (128, 512), 0) >> 6) == (lax.broadcasted_iota(jnp.int32, (128, 512), 1) >> 8)

    @pl.when(is_first == 1)
    def _():
        has = tbl_ref[step, 9].astype(F32)
        s_ssd[...] = jnp.where(gmask_s, _tile_rows(sssd_in_ref[0], 2), 0.0) * has
        s_ret[...] = jnp.where(bd, _tile_rows(sret_in_ref[0], 4), 0.0) * has
        s_hg[...] = jnp.where(bd, _tile_rows(shg_in_ref[0], 4), 0.0) * has

    dt_col = C_DTB if reverse else C_DTF
    f_col = C_HG + (768 if reverse else 512)

    x = proj_ref[:, C_XBC:C_XBC + 768]
    row = lax.broadcasted_iota(jnp.int32, (TB, 768), 0)
    prev_row = prev_ref[7:8, :] * keep_prev
    next_row = next_ref[0:1, :] * keep_next
    xm1 = jnp.where(row == 0, prev_row, pltpu.roll(x, 1, axis=0))
    xp1 = jnp.where(row == TB - 1, next_row, pltpu.roll(x, TB - 1, axis=0))
    conv = convw_ref[0:1, :] * xm1 + convw_ref[1:2, :] * x + convw_ref[2:3, :] * xp1 + convb_ref[...]
    xbc_scr[...] = _silu(conv)

    dt = _softplus(proj_ref[:, dt_col:dt_col + 512] + dtb_ref[...])
    g_scr[:, 0:512] = -dt * jnp.exp(alog_ref[...])
    b_scr[:, 768:1280] = dt

    lb = lb_ref[...]
    fr = proj_ref[:, f_col:f_col + 256]
    g_scr[:, 512:768] = jnp.log(lb + (1.0 - lb) * _sigmoid(fr))

    tri = tri_ref[...]
    for c in range(NCH):
        rows = slice(c * CHUNK, (c + 1) * CHUNK)
        hi, mid, lo = _split3(g_scr[rows, :])
        b_scr[rows, 0:768] = _dot(tri, hi) + _dot(tri, mid) + _dot(tri, lo)

    gmask_b = (lax.broadcasted_iota(jnp.int32, (512, 128), 0) >> 8) == (lax.broadcasted_iota(jnp.int32, (512, 128), 1) >> 6)
    t512 = lax.broadcasted_iota(jnp.int32, (CHUNK, 512), 0)
    s512 = lax.broadcasted_iota(jnp.int32, (CHUNK, 512), 1) & 63
    t256 = lax.broadcasted_iota(jnp.int32, (CHUNK, 256), 0)
    s256 = lax.broadcasted_iota(jnp.int32, (CHUNK, 256), 1) & 63
    if reverse:
        causal512, causal256 = s512 >= t512, s256 >= t256
    else:
        causal512, causal256 = s512 <= t512, s256 <= t256
    diag512 = s512 == t512
    sl8 = lax.broadcasted_iota(jnp.int32, (SUB, 256), 1) & 63
    half_lo = (lax.broadcasted_iota(jnp.int32, (CHUNK, 256), 1) & 63) < 32
    end_row = 0 if reverse else CHUNK - 1

    def chunk_body(k, carry):
        c = (NCH - 1 - k) if reverse else k
        r0 = pl.multiple_of(c * CHUNK, CHUNK)
        rows = pl.ds(r0, CHUNK)

        bb = b_scr[rows, 0:512]
        dtc = b_scr[rows, 768:1280]
        xs = xbc_scr[rows, 0:512]
        bm = xbc_scr[rows, 512:640].astype(BF16)
        cm = xbc_scr[rows, 640:768].astype(BF16)
        xdt = xs * dtc
        bend = b_scr[pl.ds(r0 + end_row, 1), 0:512]
        rvec = jnp.sum(jnp.where(diag512, bb, 0.0), axis=0, keepdims=True)
        dmat = jnp.where(causal512, jnp.exp(jnp.minimum(bb - rvec, 0.0)), 0.0)
        rhs_b = jnp.where(gmask_b, _tile_rows(bm, 8), jnp.zeros((), BF16))
        a_ssd = (_dot_nt(cm, rhs_b) * dmat).astype(BF16)
        xdt_b = xdt.astype(BF16)
        o_parts = []
        for g in range(SSD_GROUPS):
            vbd = jnp.where(bd, _tile_rows(xdt_b[:, g * 256:(g + 1) * 256], 4), jnp.zeros((), BF16))
            o_parts.append(_dot(a_ssd[:, g * 256:(g + 1) * 256], vbd))
        o_ssd = jnp.concatenate(o_parts, axis=1) + jnp.exp(bb) * _dot(cm, s_ssd[...].astype(BF16))
        u = _dot_tn(bm, (xdt * jnp.exp(bend - bb)).astype(BF16))
        s_ssd[...] = s_ssd[...] * jnp.exp(bend) + jnp.where(gmask_s, u, 0.0)
        o_scr[rows, 256:768] = o_ssd

        cos = cos_ref[rows, :]
        sin = sin_ref[rows, :]

        def rope(a):
            ar = jnp.where(half_lo, pltpu.roll(a, 224, axis=1), pltpu.roll(a, 32, axis=1))
            return a * cos + ar * sin

        q = rope(proj_ref[rows, C_RET:C_RET + 256])
        kk = rope(proj_ref[rows, C_RET + 256:C_RET + 512] * (HEAD_DIM ** -0.5))
        v = proj_ref[rows, C_RET + 512:C_RET + 768].astype(BF16)
        kbd = jnp.where(bd, _tile_rows(kk.astype(BF16), 4), jnp.zeros((), BF16))
        a_ret = (_dot_nt(q.astype(BF16), kbd) * retd_ref[...]).astype(BF16)
        vbd = jnp.where(bd, _tile_rows(v, 4), jnp.zeros((), BF16))
        o_ret = _dot(a_ret, vbd) + _dot((q * reteq_ref[...]).astype(BF16), s_ret[...].astype(BF16))
        u = _dot_tn((kk * retek_ref[...]).astype(BF16), v)
        s_ret[...] = s_ret[...] * reteend_ref[...] + jnp.where(bd, u, 0.0)
        o_scr[rows, 768:1024] = o_ret

        bb = b_scr[rows, 512:768]
        q = _silu(proj_ref[rows, C_HG:C_HG + 256]) * (HEAD_DIM ** -0.5)
        v = proj_ref[rows, C_HG + 256:C_HG + 512].astype(BF16)
        key = (1.0 - lb) * _sigmoid(-proj_ref[rows, f_col:f_col + 256])
        bend = b_scr[pl.ds(r0 + end_row, 1), 512:768]

        def brow(r):
            return b_scr[pl.ds(r0 + r, 1), 512:768]

        zero_row = jnp.zeros((1, 256), F32)
        if reverse:
            rs = [brow(SUB * i + SUB) if i < NSUB - 1 else zero_row for i in range(NSUB)]
            re = [brow(SUB * j) for j in range(NSUB)]
        else:
            rs = [brow(SUB * i - 1) if i > 0 else zero_row for i in range(NSUB)]
            re = [brow(SUB * j + SUB - 1) for j in range(NSUB)]
        qh = [q[SUB * i:SUB * (i + 1)] * jnp.exp(bb[SUB * i:SUB * (i + 1)] - rs[i]) for i in range(NSUB)]
        kh = jnp.concatenate(
            [key[SUB * j:SUB * (j + 1)] * jnp.exp(re[j] - bb[SUB * j:SUB * (j + 1)]) for j in range(NSUB)], axis=0)
        pairs = [(i, j) for i in range(NSUB) for j in range(NSUB) if (j >= i if reverse else j <= i)]
        lhs = jnp.concatenate(
            [qh[i] * jnp.exp(jnp.minimum(rs[i] - re[j], EXP_CLAMP)) for (i, j) in pairs], axis=0).astype(BF16)
        kbd = jnp.where(bd, _tile_rows(kh.astype(BF16), 4), jnp.zeros((), BF16))
        prod = _dot_nt(lhs, kbd)
        a_tiles = []
        for i in range(NSUB):
            a_i = jnp.zeros((SUB, 256), F32)
            for p, (pi, pj) in enumerate(pairs):
                if pi == i:
                    a_i = jnp.where((sl8 >> 3) == pj, prod[SUB * p:SUB * (p + 1)], a_i)
            a_tiles.append(a_i)
        a_hg = jnp.where(causal256, jnp.concatenate(a_tiles, axis=0), 0.0).astype(BF16)
        vbd = jnp.where(bd, _tile_rows(v, 4), jnp.zeros((), BF16))
        o_hg = _dot(a_hg, vbd) + _dot_nt((q * jnp.exp(bb)).astype(BF16), s_hg[...].astype(BF16))
        u = _dot_tn(v, (key * jnp.exp(bend - bb)).astype(BF16))
        s_hg[...] = s_hg[...] * jnp.exp(bend) + jnp.where(bd, u, 0.0)
        o_scr[rows, 0:256] = o_hg
        return carry

    lax.fori_loop(0, NCH, chunk_body, 0)

    sssd_out_ref[0] = s_ssd[0:64, :] + s_ssd[64:128, :]
    sret_out_ref[0] = s_ret[0:64, :] + s_ret[64:128, :] + s_ret[128:192, :] + s_ret[192:256, :]
    shg_out_ref[0] = s_hg[0:64, :] + s_hg[64:128, :] + s_hg[128:192, :] + s_hg[192:256, :]

    if reverse:
        y_ref[...] = o_scr[...]
    else:
        ones_bd = bd.astype(BF16)
        o_hg = o_scr[:, 0:256] + obwd_ref[:, 0:256]
        ms = _seg_mean(o_hg * o_hg, ones_bd)
        gate = _silu(proj_ref[:, C_HG + 1024:C_HG + 1280])
        y_ref[:, 0:256] = (o_hg * lax.rsqrt(ms + EPS) * hgn_ref[...] * gate).astype(y_ref.dtype)

        y = o_scr[:, 256:768] + obwd_ref[:, 256:768] + xbc_scr[:, 0:512] * dskip_ref[...]
        y = y * _silu(proj_ref[:, C_Z:C_Z + 512])
        y = y * lax.rsqrt(jnp.mean(y * y, axis=-1, keepdims=True) + EPS) * ssdn_ref[...]
        y_ref[:, 256:768] = y.astype(y_ref.dtype)

        o_ret = o_scr[:, 768:1024] + obwd_ref[:, 768:1024]
        xc = o_ret - _seg_mean(o_ret, ones_bd)
        var = _seg_mean(xc * xc, ones_bd)
        gate = _silu(proj_ref[:, C_RET + 768:C_RET + 1024])
        y_ref[:, 768:1024] = (xc * lax.rsqrt(var + EPS) * retn_ref[...] * gate).astype(y_ref.dtype)


def _block_table(seq_lens, reverse):
    rows = []
    blk0 = 0
    max_blocks = max(t // TB for t in seq_lens)
    for s, t in enumerate(seq_lens):
        nb = t // TB
        for k in range(nb):
            pos = (nb - 1 - k) if reverse else k
            rows.append([blk0 + pos, s, int(k == 0), int(pos == 0), int(pos == nb - 1), pos])
        blk0 += nb
    return np.asarray(rows, np.int32), blk0, max_blocks


def _tri_matrix(reverse):
    t = np.arange(CHUNK)
    m = (t[None, :] >= t[:, None]) if reverse else (t[None, :] <= t[:, None])
    return jnp.asarray(m, BF16)


def _retention_constants(reverse):
    offset = RET_BWD_OFFSET if reverse else 0.0
    ld = np.log1p(-(2.0 ** (-5.0 - offset - np.arange(RET_HEADS, dtype=np.float64))))
    ld = np.repeat(ld, HEAD_DIM)[None, :]
    t = np.arange(CHUNK, dtype=np.float64)[:, None]
    s = (np.arange(256) % HEAD_DIM)[None, :].astype(np.float64)
    if reverse:
        b = (CHUNK - t) * ld
        dmat = np.where(s >= t, np.exp((s - t) * ld), 0.0)
    else:
        b = (t + 1.0) * ld
        dmat = np.where(s <= t, np.exp((t - s) * ld), 0.0)
    bend = CHUNK * ld
    f = lambda a: jnp.asarray(a, F32)
    return f(dmat), f(np.exp(b)), f(np.exp(bend - b)), f(np.exp(bend))


def _rope_tables(seq_lens, use_rope):
    n_pos = max(t for t, r in zip(seq_lens, use_rope) if r) if any(use_rope) else 0
    n_freq = HEAD_DIM // 4
    tok = np.arange(n_pos)
    rowp = (tok // GRID_W).astype(np.float32)
    colp = (tok % GRID_W).astype(np.float32)
    inv_freq = (ROPE_BASE ** (-np.arange(n_freq, dtype=np.float32) / n_freq)).astype(np.float32)
    ang = np.concatenate([rowp[:, None] * inv_freq, colp[:, None] * inv_freq], axis=-1)
    cos, sin = jnp.cos(jnp.asarray(ang)), jnp.sin(jnp.asarray(ang))
    cos_h = jnp.concatenate([cos, cos], axis=-1)
    sin_h = jnp.concatenate([-sin, sin], axis=-1)
    cos_t = jnp.concatenate([jnp.tile(cos_h, (1, RET_HEADS)), jnp.ones((TB, 256), F32)], axis=0)
    sin_t = jnp.concatenate([jnp.tile(sin_h, (1, RET_HEADS)), jnp.zeros((TB, 256), F32)], axis=0)
    return cos_t, sin_t, n_pos // TB


def mixer_direction(proj, seq_lens, use_rope, params, states, obwd, reverse):
    n_tok = proj.shape[0]
    n_seq = len(seq_lens)
    n_state = states[0].shape[0]
    tbl, n_blocks, _ = _block_table(seq_lens, reverse)
    cos_t, sin_t, ident_blk = _rope_tables(seq_lens, use_rope)
    rope_blk = np.asarray([tbl[i, 5] if use_rope[tbl[i, 1]] else ident_blk for i in range(n_blocks)], np.int32)
    tbl = np.concatenate([tbl[:, :5], rope_blk[:, None],
                          np.maximum(tbl[:, 0:1] * (TB // 8) - 1, 0),
                          np.minimum((tbl[:, 0:1] + 1) * (TB // 8), n_tok // 8 - 1),
                          np.minimum(tbl[:, 1:2], n_state - 1),
                          (tbl[:, 1:2] < n_state).astype(np.int32)], axis=1).astype(np.int32)
    d = 1 if reverse else 0
    retd, reteq, retek, reteend = _retention_constants(reverse)
    rep = lambda a: jnp.repeat(a.astype(F32), HEAD_DIM)[None, :]
    small = [
        _tri_matrix(reverse),
        params["conv_w"].astype(F32), params["conv_b"].astype(F32)[None, :],
        rep(params["dt_bias"][d]), rep(params["a_log"][d]), rep(params["d_skip"]),
        params["lb"][d].astype(F32)[None, :],
        params["hg_norm_g"].astype(F32)[None, :], params["ssd_norm_g"].astype(F32)[None, :],
        params["ret_norm_g"].astype(F32)[None, :],
        retd, reteq, retek, reteend,
    ]
    full = lambda a: pl.BlockSpec(a.shape, lambda i, t: (0,) * a.ndim)
    in_specs = [
        pl.BlockSpec((TB, P_COLS), lambda i, t: (t[i, 0], 0)),
        pl.BlockSpec((8, 768), lambda i, t: (t[i, 6], 0)),
        pl.BlockSpec((8, 768), lambda i, t: (t[i, 7], 0)),
        pl.BlockSpec((TB, 256), lambda i, t: (t[i, 5], 0)),
        pl.BlockSpec((TB, 256), lambda i, t: (t[i, 5], 0)),
    ] + [full(a) for a in small] + [
        pl.BlockSpec((1, 64, 512), lambda i, t: (t[i, 8], 0, 0)),
        pl.BlockSpec((1, 64, 256), lambda i, t: (t[i, 8], 0, 0)),
        pl.BlockSpec((1, 64, 256), lambda i, t: (t[i, 8], 0, 0)),
    ]
    args = [proj, proj, proj, cos_t, sin_t] + small + list(states)
    if not reverse:
        in_specs.append(pl.BlockSpec((TB, 1024), lambda i, t: (t[i, 0], 0)))
        args.append(obwd)
    out_dtype = F32 if reverse else BF16
    out_shape = [
        jax.ShapeDtypeStruct((n_tok, 1024), out_dtype),
        jax.ShapeDtypeStruct((n_seq, 64, 512), F32),
        jax.ShapeDtypeStruct((n_seq, 64, 256), F32),
        jax.ShapeDtypeStruct((n_seq, 64, 256), F32),
    ]
    out_specs = [
        pl.BlockSpec((TB, 1024), lambda i, t: (t[i, 0], 0)),
        pl.BlockSpec((1, 64, 512), lambda i, t: (t[i, 1], 0, 0)),
        pl.BlockSpec((1, 64, 256), lambda i, t: (t[i, 1], 0, 0)),
        pl.BlockSpec((1, 64, 256), lambda i, t: (t[i, 1], 0, 0)),
    ]
    scratch = [
        pltpu.VMEM((TB, 768), F32),
        pltpu.VMEM((TB, 768), F32),
        pltpu.VMEM((TB, 1280), F32),
        pltpu.VMEM((TB, 1024), F32),
        pltpu.VMEM((128, 512), F32),
        pltpu.VMEM((256, 256), F32),
        pltpu.VMEM((256, 256), F32),
    ]
    return pl.pallas_call(
        functools.partial(_mixer_kernel, reverse=reverse),
        grid_spec=pltpu.PrefetchScalarGridSpec(
            num_scalar_prefetch=1, grid=(n_blocks,), in_specs=in_specs, out_specs=out_specs,
            scratch_shapes=scratch),
        out_shape=out_shape,
        compiler_params=pltpu.CompilerParams(dimension_semantics=("arbitrary",), vmem_limit_bytes=48 * 1024 * 1024),
        name="mixer_bwd" if reverse else "mixer_fwd",
    )(jnp.asarray(tbl), *args)


ADA_TN = 1024


def _ada_kernel(c_ref, w_ref, b_ref, o_ref):
    s = _silu(c_ref[...]).astype(BF16)
    o_ref[0] = _dot(s, w_ref[0].astype(BF16)) + b_ref[0]


def ada_modulation(cond, w_ada, b_ada):
    n_c = cond.shape[0]
    n_l, d, n_out = w_ada.shape
    return pl.pallas_call(
        _ada_kernel,
        grid=(n_l, n_out // ADA_TN),
        in_specs=[pl.BlockSpec((n_c, d), lambda l, j: (0, 0)),
                  pl.BlockSpec((1, d, ADA_TN), lambda l, j: (l, 0, j)),
                  pl.BlockSpec((1, 1, ADA_TN), lambda l, j: (l, 0, j))],
        out_specs=pl.BlockSpec((1, n_c, ADA_TN), lambda l, j: (l, 0, j)),
        out_shape=jax.ShapeDtypeStruct((n_l, n_c, n_out), F32),
        name="ada_mod",
    )(cond, w_ada, b_ada.reshape(n_l, 1, n_out))


TM = 256


def _rms(x):
    return x * lax.rsqrt(jnp.mean(x * x, axis=-1, keepdims=True) + EPS)


def _inproj_kernel(crow_ref, x_ref, mod_ref, g_ref, w_ref, o_ref):
    r = crow_ref[pl.program_id(0)]
    sh = mod_ref[pl.ds(r, 1), 0:1024]
    sc = mod_ref[pl.ds(r, 1), 1024:2048]
    h = _rms(x_ref[...]) * g_ref[...] * (1.0 + sc) + sh
    o_ref[...] = _dot(h.astype(BF16), w_ref[...])


def in_projection(x, crow, mod, gain, w):
    n_tok, d = x.shape
    p = w.shape[1]
    return pl.pallas_call(
        _inproj_kernel,
        grid_spec=pltpu.PrefetchScalarGridSpec(
            num_scalar_prefetch=1, grid=(n_tok // TM,),
            in_specs=[pl.BlockSpec((TM, d), lambda i, c: (i, 0)),
                      pl.BlockSpec(mod.shape, lambda i, c: (0, 0)),
                      pl.BlockSpec((1, d), lambda i, c: (0, 0)),
                      pl.BlockSpec((d, p), lambda i, c: (0, 0))],
            out_specs=pl.BlockSpec((TM, p), lambda i, c: (i, 0))),
        out_shape=jax.ShapeDtypeStruct((n_tok, p), F32),
        compiler_params=pltpu.CompilerParams(dimension_semantics=("arbitrary",), vmem_limit_bytes=48 * 1024 * 1024),
        name="in_proj",
    )(crow, x, mod, gain[None, :], w)


def _outproj_kernel(crow_ref, y_ref, x_ref, mod_ref, g_ref, wo_ref, wr_ref, x1_ref, aff_ref):
    r = crow_ref[pl.program_id(0)]
    g1 = mod_ref[pl.ds(r, 1), 2048:3072]
    sh = mod_ref[pl.ds(r, 1), 3072:4096]
    sc = mod_ref[pl.ds(r, 1), 4096:5120]
    x1 = x_ref[...] + g1 * _dot(y_ref[...], wo_ref[...])
    x1_ref[...] = x1
    h = _rms(x1) * g_ref[...] * (1.0 + sc) + sh
    lt = _dot_nt(wr_ref[...], h.astype(BF16))
    e = jnp.exp(lt - jnp.max(lt, axis=0, keepdims=True))
    aff_ref[...] = e / jnp.sum(e, axis=0, keepdims=True)


def out_projection(y, x, crow, mod, gain, w_out, w_router):
    n_tok, d = x.shape
    wr = w_router.astype(BF16).T
    return pl.pallas_call(
        _outproj_kernel,
        grid_spec=pltpu.PrefetchScalarGridSpec(
            num_scalar_prefetch=1, grid=(n_tok // TM,),
            in_specs=[pl.BlockSpec((TM, d), lambda i, c: (i, 0)),
                      pl.BlockSpec((TM, d), lambda i, c: (i, 0)),
                      pl.BlockSpec(mod.shape, lambda i, c: (0, 0)),
                      pl.BlockSpec((1, d), lambda i, c: (0, 0)),
                      pl.BlockSpec((d, d), lambda i, c: (0, 0)),
                      pl.BlockSpec((N_EXPERTS, d), lambda i, c: (0, 0))],
            out_specs=[pl.BlockSpec((TM, d), lambda i, c: (i, 0)),
                       pl.BlockSpec((N_EXPERTS, TM), lambda i, c: (0, i))]),
        out_shape=[jax.ShapeDtypeStruct((n_tok, d), F32), jax.ShapeDtypeStruct((N_EXPERTS, n_tok), F32)],
        compiler_params=pltpu.CompilerParams(dimension_semantics=("arbitrary",)),
        name="out_proj",
    )(crow, y, x, mod, gain[None, :], w_out, wr)


TK_BLK = 256
TK_JB = 64


def _topk_kernel(aff_ref, tri_ref, idx_ref, gate_ref, csum_scr, sel_scr, ti_scr, tg_scr, *, n, cap, cpad):
    a = aff_ref[...]

    def count_ge(t):
        return jnp.sum((a >= t).astype(jnp.int32), axis=1, keepdims=True)

    def search(it, lo):
        cand = lo | (1 << (30 - it))
        return jnp.where(count_ge(pltpu.bitcast(cand, F32)) >= cap, cand, lo)

    lo_bits = lax.fori_loop(0, 31, search, jnp.zeros((N_EXPERTS, 1), jnp.int32))

    def refine(it, lh):
        lo, hi = lh
        mid = 0.5 * (lo + hi)
        ok = count_ge(mid) >= cap
        return jnp.where(ok, mid, lo), jnp.where(ok, hi, mid)

    lo_v, hi_v = lax.fori_loop(0, 32, refine, (pltpu.bitcast(lo_bits, F32), pltpu.bitcast(lo_bits + 1, F32)))
    gt = a >= hi_v
    eq = (a >= lo_v) & (a < hi_v)
    need = (cap - jnp.sum(gt.astype(jnp.int32), axis=1, keepdims=True)).astype(F32)

    tri = tri_ref[...]

    def prefix(mask):
        carry = jnp.zeros((N_EXPERTS, 1), F32)
        parts = []
        for kb in range(n // TK_BLK):
            blk = mask[:, kb * TK_BLK:(kb + 1) * TK_BLK].astype(BF16)
            loc = _dot(blk, tri) + carry
            parts.append(loc)
            carry = loc[:, TK_BLK - 1:TK_BLK]
        return jnp.concatenate(parts, axis=1)

    sel = gt | (eq & (prefix(eq) <= need))
    csum_scr[...] = prefix(sel)
    sel_scr[...] = jnp.where(sel, a, 0.0)
    lane = lax.broadcasted_iota(jnp.int32, (TK_JB, 128), 1)

    def per_expert(e, carry):
        cs = csum_scr[pl.ds(e, 1), :]
        av = sel_scr[pl.ds(e, 1), :]

        def per_block(jb, c2):
            j0 = pl.multiple_of(jb * TK_JB, TK_JB)
            j = (lax.broadcasted_iota(jnp.int32, (TK_JB, n), 0) + j0).astype(F32)
            idx_col = jnp.sum((cs <= j).astype(F32), axis=1, keepdims=True)
            gate_col = jnp.sum(jnp.where(cs == j + 1.0, av, 0.0), axis=1, keepdims=True)
            rows = pl.ds(j0, TK_JB)
            ti_scr[rows, :] = jnp.where(lane == e, idx_col, ti_scr[rows, :])
            tg_scr[rows, :] = jnp.where(lane == e, gate_col, tg_scr[rows, :])
            return c2

        return lax.fori_loop(0, cpad // TK_JB, per_block, carry)

    lax.fori_loop(0, N_EXPERTS, per_expert, 0)
    idx_ref[...] = ti_scr[...].T[0:N_EXPERTS, :].astype(jnp.int32)
    gate_ref[...] = tg_scr[...].T[0:N_EXPERTS, :]


def expert_choice_topk(aff_t, tok0, n_rows, n):
    cap = EC_CAPACITY_FACTOR * n // N_EXPERTS
    cpad = max(cap, 128)
    blk0 = tok0 // n
    t = np.arange(TK_BLK)
    tri = jnp.asarray(t[:, None] <= t[None, :], BF16)
    idx, gate = pl.pallas_call(
        functools.partial(_topk_kernel, n=n, cap=cap, cpad=cpad),
        grid=(n_rows,),
        in_specs=[pl.BlockSpec((N_EXPERTS, n), lambda r: (0, blk0 + r)),
                  pl.BlockSpec((TK_BLK, TK_BLK), lambda r: (0, 0))],
        out_specs=[pl.BlockSpec((None, N_EXPERTS, cpad), lambda r: (r, 0, 0)),
                   pl.BlockSpec((None, N_EXPERTS, cpad), lambda r: (r, 0, 0))],
        out_shape=[jax.ShapeDtypeStruct((n_rows, N_EXPERTS, cpad), jnp.int32),
                   jax.ShapeDtypeStruct((n_rows, N_EXPERTS, cpad), F32)],
        scratch_shapes=[pltpu.VMEM((N_EXPERTS, n), F32), pltpu.VMEM((N_EXPERTS, n), F32),
                        pltpu.VMEM((cpad, 128), F32), pltpu.VMEM((cpad, 128), F32)],
        compiler_params=pltpu.CompilerParams(dimension_semantics=("arbitrary",)),
        name=f"topk_{n}",
    )(aff_t, tri)
    return idx[:, :, :cap], gate[:, :, :cap]


MOE_NT = 4096
MOE_C = 512
MOE_GROUP = 8


def _moe_kernel(crow_ref, idx_ref, gate_ref, x_ref, mod_ref, g_ref, fg_ref, wg_ref, wu_ref, wd_ref, o_ref,
                idx_smem, gate_smem, xs_scr, ys_scr, *, final):
    grp = pl.program_id(0)
    e = pl.program_id(1)
    r = crow_ref[grp]

    @pl.when(e == 0)
    def _():
        o_ref[...] = x_ref[...]

    pltpu.sync_copy(idx_ref.at[0], idx_smem)
    pltpu.sync_copy(gate_ref.at[0], gate_smem)

    def gather(j, c):
        xs_scr[pl.ds(j, 1), :] = x_ref[0, pl.ds(idx_smem[0, j], 1), :]
        return c

    lax.fori_loop(0, MOE_C, gather, 0, unroll=8)

    sh = mod_ref[pl.ds(r, 1), 3072:4096]
    sc = mod_ref[pl.ds(r, 1), 4096:5120]
    g2 = mod_ref[pl.ds(r, 1), 5120:6144]
    h = (_rms(xs_scr[...]) * g_ref[...] * (1.0 + sc) + sh).astype(BF16)
    hid = _silu(_dot(h, wg_ref[0])) * _dot(h, wu_ref[0])
    ys_scr[...] = _dot(hid.astype(BF16), wd_ref[0]) * g2

    def scatter(grp_j, c):
        j0 = grp_j * MOE_GROUP
        toks = [idx_smem[0, j0 + k] for k in range(MOE_GROUP)]
        vals = [o_ref[0, pl.ds(toks[k], 1), :] + gate_smem[0, j0 + k] * ys_scr[pl.ds(j0 + k, 1), :]
                for k in range(MOE_GROUP)]
        for k in range(MOE_GROUP):
            o_ref[0, pl.ds(toks[k], 1), :] = vals[k]
        return c

    lax.fori_loop(0, MOE_C // MOE_GROUP, scatter, 0)

    if final:
        @pl.when(e == N_EXPERTS - 1)
        def _():
            o_ref[0] = _rms(o_ref[0]) * fg_ref[...]


def moe_ffn(x1, idx, gate, crow_grp, mod, gain, final_gain, w_gate, w_up, w_down, final):
    n_tok, d = x1.shape
    n_grp = n_tok // MOE_NT
    xg = x1.reshape(n_grp, MOE_NT, d)
    out = pl.pallas_call(
        functools.partial(_moe_kernel, final=final),
        grid_spec=pltpu.PrefetchScalarGridSpec(
            num_scalar_prefetch=1, grid=(n_grp, N_EXPERTS),
            in_specs=[pl.BlockSpec((1, 1, MOE_C), lambda g, e, c: (g * N_EXPERTS + e, 0, 0)),
                      pl.BlockSpec((1, 1, MOE_C), lambda g, e, c: (g * N_EXPERTS + e, 0, 0)),
                      pl.BlockSpec((1, MOE_NT, d), lambda g, e, c: (g, 0, 0), pipeline_mode=pl.Buffered(1)),
                      pl.BlockSpec(mod.shape, lambda g, e, c: (0, 0)),
                      pl.BlockSpec((1, d), lambda g, e, c: (0, 0)),
                      pl.BlockSpec((1, d), lambda g, e, c: (0, 0)),
                      pl.BlockSpec((1, d, D_EXPERT), lambda g, e, c: (e, 0, 0)),
                      pl.BlockSpec((1, d, D_EXPERT), lambda g, e, c: (e, 0, 0)),
                      pl.BlockSpec((1, D_EXPERT, d), lambda g, e, c: (e, 0, 0))],
            out_specs=pl.BlockSpec((1, MOE_NT, d), lambda g, e, c: (g, 0, 0), pipeline_mode=pl.Buffered(1)),
            scratch_shapes=[pltpu.SMEM((1, MOE_C), jnp.int32), pltpu.SMEM((1, MOE_C), F32),
                            pltpu.VMEM((MOE_C, d), F32), pltpu.VMEM((MOE_C, d), F32)]),
        out_shape=jax.ShapeDtypeStruct((n_grp, MOE_NT, d), F32),
        compiler_params=pltpu.CompilerParams(dimension_semantics=("arbitrary", "arbitrary"),
                                             vmem_limit_bytes=56 * 1024 * 1024),
        name="moe_ffn",
    )(crow_grp, idx, gate, xg, mod, gain[None, :], final_gain[None, :], w_gate, w_up, w_down)
    return out.reshape(n_tok, d)


def _ssd_to_k(s):
    return s.transpose(0, 2, 1, 3).reshape(s.shape[0], 64, 512)


def _ssd_from_k(m):
    return m.reshape(m.shape[0], 64, 8, 64).transpose(0, 2, 1, 3)


def _ret_to_k(s):
    return s.transpose(0, 2, 1, 3).reshape(s.shape[0], 64, 256)


def _ret_from_k(m):
    return m.reshape(m.shape[0], 64, 4, 64).transpose(0, 2, 1, 3)


def _hg_to_k(s):
    return s.transpose(0, 3, 1, 2).reshape(s.shape[0], 64, 256)


def _hg_from_k(m):
    return m.reshape(m.shape[0], 64, 4, 64).transpose(0, 2, 3, 1)


def _arrange_w_in(w):
    hg = w[:, 0:1280]
    z = w[:, 1280:1792]
    xbc = w[:, 1792:2560]
    dtf = jnp.repeat(w[:, 2560:2568], HEAD_DIM, axis=1)
    dtb = jnp.repeat(w[:, 2568:2576], HEAD_DIM, axis=1)
    ret = w[:, 2576:3600]
    return jnp.concatenate([xbc, z, dtf, dtb, hg, ret], axis=1).astype(BF16)


def kernel(x_prompt, x_sample, state_hgrn, state_ssd, state_ret, c, c_ctx, norm1_g, norm2_g, final_g,
           w_ada, b_ada, w_in, w_out, hg_lb, hg_norm_g, ssd_conv_w, ssd_conv_b, ssd_dt_bias, ssd_a_log,
           ssd_d, ssd_norm_g, ret_norm_g, w_router, w_gate, w_up, w_down):
    n_ctx, t_ctx, d = x_prompt.shape
    n_smp, t_smp, _ = x_sample.shape
    n_tok_s = n_smp * t_smp
    n_tok = n_tok_s + n_ctx * t_ctx
    seq_lens = [t_smp] * n_smp + [t_ctx] * n_ctx
    use_rope = [True] * n_smp + [False] * n_ctx

    x = jnp.concatenate([x_sample.reshape(n_tok_s, d), x_prompt.reshape(n_ctx * t_ctx, d)], axis=0)
    cond = jnp.zeros((16, d), F32).at[0].set(c_ctx).at[1:1 + n_smp].set(c)
    crow_tm = jnp.asarray([i * TM // t_smp + 1 if i * TM < n_tok_s else 0 for i in range(n_tok // TM)], jnp.int32)
    crow_grp = jnp.asarray([g + 1 if g < n_smp else 0 for g in range(n_tok // MOE_NT)], jnp.int32)

    lb_p = jax.nn.softmax(hg_lb.astype(F32), axis=0)
    lb_all = jnp.cumsum(lb_p, axis=0) - lb_p[0]
    mod_all = ada_modulation(cond, w_ada, b_ada)

    rows_per_grp = MOE_NT // t_ctx
    n_grp_ctx = n_ctx // rows_per_grp
    new_states = {"hg": [], "ssd": [], "ret": []}
    for l in range(DEPTH):
        mod = mod_all[l]
        proj = in_projection(x, crow_tm, mod, norm1_g[l], _arrange_w_in(w_in[l]))
        params = dict(conv_w=ssd_conv_w[l], conv_b=ssd_conv_b[l], dt_bias=ssd_dt_bias[l], a_log=ssd_a_log[l],
                      d_skip=ssd_d[l], lb=lb_all[l], hg_norm_g=hg_norm_g[l], ssd_norm_g=ssd_norm_g[l],
                      ret_norm_g=ret_norm_g[l])
        outs = []
        for dirn in (1, 0):
            st = (_ssd_to_k(state_ssd[:, l, dirn]), _ret_to_k(state_ret[:, l, dirn]), _hg_to_k(state_hgrn[:, l, dirn]))
            outs.append(mixer_direction(proj, seq_lens, use_rope, params, st, outs[0][0] if outs else None, dirn == 1))
        (_, b_ssd, b_ret, b_hg), (y, f_ssd, f_ret, f_hg) = outs
        new_states["ssd"].append(jnp.stack([_ssd_from_k(f_ssd[n_smp:]), _ssd_from_k(b_ssd[n_smp:])], axis=1))
        new_states["ret"].append(jnp.stack([_ret_from_k(f_ret[n_smp:]), _ret_from_k(b_ret[n_smp:])], axis=1))
        new_states["hg"].append(jnp.stack([_hg_from_k(f_hg[n_smp:]), _hg_from_k(b_hg[n_smp:])], axis=1))

        x1, aff_t = out_projection(y, x, crow_tm, mod, norm2_g[l], w_out[l].astype(BF16), w_router[l])
        idx_s, gate_s = expert_choice_topk(aff_t, 0, n_smp, t_smp)
        idx_c, gate_c = expert_choice_topk(aff_t, n_tok_s, n_ctx, t_ctx)
        cap_c = idx_c.shape[-1]
        idx_c = idx_c + (jnp.arange(n_ctx, dtype=jnp.int32) % rows_per_grp)[:, None, None] * t_ctx
        regroup = lambda a: a.reshape(n_grp_ctx, rows_per_grp, N_EXPERTS, cap_c).transpose(0, 2, 1, 3).reshape(
            n_grp_ctx, N_EXPERTS, rows_per_grp * cap_c)
        idx = jnp.concatenate([idx_s, regroup(idx_c)], axis=0).reshape(-1, 1, MOE_C)
        gate = jnp.concatenate([gate_s, regroup(gate_c)], axis=0).reshape(-1, 1, MOE_C)
        x = moe_ffn(x1, idx, gate, crow_grp, mod, norm2_g[l], final_g, w_gate[l].astype(BF16),
                    w_up[l].astype(BF16), w_down[l].astype(BF16), final=(l == DEPTH - 1))

    y_sample = x[:n_tok_s].reshape(n_smp, t_smp, d)
    y_prompt = x[n_tok_s:].reshape(n_ctx, t_ctx, d)
    return (y_prompt, y_sample, jnp.stack(new_states["hg"], axis=1), jnp.stack(new_states["ssd"], axis=1),
            jnp.stack(new_states["ret"], axis=1))
```

```python
import functools
import math

import jax
import jax.numpy as jnp
import numpy as np
from jax import lax
from jax.experimental import pallas as pl
from jax.experimental.pallas import tpu as pltpu

D_MODEL = 1024
DEPTH = 2
GRID_W = 64
CHUNK = 64
EPS = 1e-6
HEAD_DIM = 64
HG_W = 256
SSD_W = 512
RET_W = 256
HG_HEADS = 4
SSD_HEADS = 8
SSD_N = 64
SSD_GROUPS = 2
SSD_CONV_CH = 768
RET_HEADS = 4
RET_BWD_OFFSET = 0.5
ROPE_BASE = 10000.0
N_EXPERTS = 16
EC_CAPACITY_FACTOR = 2
D_EXPERT = 512

R_XBC = 0
R_Z = 768
R_DT = 1280
R_HG = 1408
R_RET = 2688
R_COLS = 3712

C_XBC = 0
C_Z = 768
C_DT = 1280
C_LA = 1408
C_HG = 1536
C_RET = 3328
P_COLS = 4352

TB = 256
NCH = TB // CHUNK
SUB = 8
NSUB = CHUNK // SUB
EXP_CLAMP = 80.0

F32 = jnp.float32
BF16 = jnp.bfloat16


def _sigmoid(x):
    return 1.0 / (1.0 + jnp.exp(-x))


def _silu(x):
    return x * (0.5 + 0.5 * jnp.tanh(0.5 * x))


def _softplus(x):
    return jnp.maximum(x, 0.0) + jnp.log1p(jnp.exp(-jnp.abs(x)))


def _dot(a, b):
    return jnp.dot(a, b, preferred_element_type=F32)


def _dot_nt(a, b):
    return lax.dot_general(a, b, (((1,), (1,)), ((), ())), preferred_element_type=F32)


def _dot_tn(a, b):
    return lax.dot_general(a, b, (((0,), (0,)), ((), ())), preferred_element_type=F32)


def _split3(x):
    hi = x.astype(BF16)
    r = x - hi.astype(F32)
    mid = r.astype(BF16)
    lo = (r - mid.astype(F32)).astype(BF16)
    return hi, mid, lo


def _seg_mean(x, ones_bd):
    hi = x.astype(BF16)
    lo = (x - hi.astype(F32)).astype(BF16)
    return (_dot(hi, ones_bd) + _dot(lo, ones_bd)) * (1.0 / HEAD_DIM)


def _tile_rows(x, n):
    return jnp.concatenate([x] * n, axis=0)


def _mixer_kernel(tbl_ref, proj_ref, prev_ref, next_ref, cos_ref, sin_ref, tri_ref, expand_ref,
                  convw_ref, convb_ref, dskip_ref,
                  hgn_ref, ssdn_ref, retn_ref, retd_ref, reteq_ref, retek_ref, reteend_ref,
                  sssd_in_ref, sret_in_ref, shg_in_ref, *rest, reverse):
    if reverse:
        (y_ref, sssd_out_ref, sret_out_ref, shg_out_ref, b_scr, s_ssd, s_ret, s_hg) = rest
        obwd_ref = None
    else:
        (obwd_ref, y_ref, sssd_out_ref, sret_out_ref, shg_out_ref, b_scr, s_ssd, s_ret, s_hg) = rest

    step = pl.program_id(0)
    is_first = tbl_ref[step, 2]
    keep_prev = (1 - tbl_ref[step, 3]).astype(F32)
    keep_next = (1 - tbl_ref[step, 4]).astype(F32)

    r256 = lax.broadcasted_iota(jnp.int32, (256, 256), 0)
    c256 = lax.broadcasted_iota(jnp.int32, (256, 256), 1)
    bd = (r256 >> 6) == (c256 >> 6)
    gmask_s = (lax.broadcasted_iota(jnp.int32, (128, 512), 0) >> 6) == (lax.broadcasted_iota(jnp.int32, (128, 512), 1) >> 8)

    @pl.when(is_first == 1)
    def _():
        has = tbl_ref[step, 9].astype(F32)
        s_ssd[...] = jnp.where(gmask_s, _tile_rows(sssd_in_ref[0], 2), 0.0) * has
        s_ret[...] = jnp.where(bd, _tile_rows(sret_in_ref[0], 4), 0.0) * has
        s_hg[...] = jnp.where(bd, _tile_rows(shg_in_ref[0], 4), 0.0) * has

    lf_col = C_HG + (768 if reverse else 512)
    key_col = C_HG + (1280 if reverse else 1024)

    prev_row = prev_ref[7:8, :] * keep_prev
    next_row = next_ref[0:1, :] * keep_next

    def split2(a):
        hi = a.astype(BF16)
        return hi, (a - hi.astype(F32)).astype(BF16)

    la_hi, la_lo = split2(proj_ref[:, C_LA:C_LA + 128])
    lf_hi, lf_lo = split2(proj_ref[:, lf_col:lf_col + 256])
    cum = _dot(tri_ref[...], jnp.concatenate([la_hi, la_lo, lf_hi, lf_lo], axis=1))
    b_scr[:, 512:768] = cum[:, 256:512] + cum[:, 512:768]
    c_hi, c_lo = split2(cum[:, 0:128] + cum[:, 128:256])
    d_hi, d_lo = split2(proj_ref[:, C_DT:C_DT + 128])
    spread = _dot(jnp.concatenate([c_hi, c_lo, d_hi, d_lo], axis=0), expand_ref[...])
    b_scr[:, 0:512] = spread[0:TB] + spread[TB:2 * TB]
    b_scr[:, 768:1280] = spread[2 * TB:3 * TB] + spread[3 * TB:4 * TB]

    gmask_b = (lax.broadcasted_iota(jnp.int32, (512, 128), 0) >> 8) == (lax.broadcasted_iota(jnp.int32, (512, 128), 1) >> 6)
    t512 = lax.broadcasted_iota(jnp.int32, (CHUNK, 512), 0)
    s512 = lax.broadcasted_iota(jnp.int32, (CHUNK, 512), 1) & 63
    t256 = lax.broadcasted_iota(jnp.int32, (CHUNK, 256), 0)
    s256 = lax.broadcasted_iota(jnp.int32, (CHUNK, 256), 1) & 63
    if reverse:
        causal512, causal256 = s512 >= t512, s256 >= t256
    else:
        causal512, causal256 = s512 <= t512, s256 <= t256
    diag512 = s512 == t512
    sl8 = lax.broadcasted_iota(jnp.int32, (SUB, 256), 1) & 63
    half_lo = (lax.broadcasted_iota(jnp.int32, (CHUNK, 256), 1) & 63) < 32
    row768 = lax.broadcasted_iota(jnp.int32, (CHUNK, 768), 0)
    ones_bd = bd.astype(BF16)
    end_row = 0 if reverse else CHUNK - 1

    def chunk_body(k, carry):
        c = (NCH - 1 - k) if reverse else k
        r0 = pl.multiple_of(c * CHUNK, CHUNK)
        rows = pl.ds(r0, CHUNK)

        x = proj_ref[rows, C_XBC:C_XBC + 768]
        first = (c == 0).astype(F32)
        last = (c == NCH - 1).astype(F32)
        prev = first * prev_row + (1.0 - first) * proj_ref[pl.ds(jnp.maximum(r0 - 1, 0), 1), C_XBC:C_XBC + 768]
        nxt = last * next_row + (1.0 - last) * proj_ref[pl.ds(jnp.minimum(r0 + CHUNK, TB - 1), 1), C_XBC:C_XBC + 768]
        xm1 = jnp.where(row768 == 0, prev, pltpu.roll(x, 1, axis=0))
        xp1 = jnp.where(row768 == CHUNK - 1, nxt, pltpu.roll(x, CHUNK - 1, axis=0))
        xbc = _silu(convw_ref[0:1, :] * xm1 + convw_ref[1:2, :] * x + convw_ref[2:3, :] * xp1 + convb_ref[...])
        bb = b_scr[rows, 0:512]
        dtc = b_scr[rows, 768:1280]
        xs = xbc[:, 0:512]
        bm = xbc[:, 512:640].astype(BF16)
        cm = xbc[:, 640:768].astype(BF16)
        xdt = xs * dtc
        bend = b_scr[pl.ds(r0 + end_row, 1), 0:512]
        rvec = jnp.sum(jnp.where(diag512, bb, 0.0), axis=0, keepdims=True)
        dmat = jnp.where(causal512, jnp.exp(jnp.minimum(bb - rvec, 0.0)), 0.0)
        rhs_b = jnp.where(gmask_b, _tile_rows(bm, 8), jnp.zeros((), BF16))
        a_ssd = (_dot_nt(cm, rhs_b) * dmat).astype(BF16)
        xdt_b = xdt.astype(BF16)
        o_parts = []
        for g in range(SSD_GROUPS):
            vbd = jnp.where(bd, _tile_rows(xdt_b[:, g * 256:(g + 1) * 256], 4), jnp.zeros((), BF16))
            o_parts.append(_dot(a_ssd[:, g * 256:(g + 1) * 256], vbd))
        o_ssd = jnp.concatenate(o_parts, axis=1) + jnp.exp(bb) * _dot(cm, s_ssd[...].astype(BF16))
        u = _dot_tn(bm, (xdt * jnp.exp(bend - bb)).astype(BF16))
        s_ssd[...] = s_ssd[...] * jnp.exp(bend) + jnp.where(gmask_s, u, 0.0)
        if reverse:
            y_ref[rows, 256:768] = o_ssd
        else:
            y = o_ssd + obwd_ref[rows, 256:768] + xs * dskip_ref[...]
            y = y * proj_ref[rows, C_Z:C_Z + 512]
            y = y * lax.rsqrt(jnp.mean(y * y, axis=-1, keepdims=True) + EPS) * ssdn_ref[...]
            y_ref[rows, 256:768] = y.astype(y_ref.dtype)

        cos = cos_ref[rows, :]
        sin = sin_ref[rows, :]

        def rope(a):
            ar = jnp.where(half_lo, pltpu.roll(a, 224, axis=1), pltpu.roll(a, 32, axis=1))
            return a * cos + ar * sin

        q = rope(proj_ref[rows, C_RET:C_RET + 256])
        kk = rope(proj_ref[rows, C_RET + 256:C_RET + 512])
        v = proj_ref[rows, C_RET + 512:C_RET + 768].astype(BF16)
        kbd = jnp.where(bd, _tile_rows(kk.astype(BF16), 4), jnp.zeros((), BF16))
        a_ret = (_dot_nt(q.astype(BF16), kbd) * retd_ref[...]).astype(BF16)
        vbd = jnp.where(bd, _tile_rows(v, 4), jnp.zeros((), BF16))
        o_ret = _dot(a_ret, vbd) + _dot((q * reteq_ref[...]).astype(BF16), s_ret[...].astype(BF16))
        u = _dot_tn((kk * retek_ref[...]).astype(BF16), v)
        s_ret[...] = s_ret[...] * reteend_ref[...] + jnp.where(bd, u, 0.0)
        if reverse:
            y_ref[rows, 768:1024] = o_ret
        else:
            o_ret = o_ret + obwd_ref[rows, 768:1024]
            xc = o_ret - _seg_mean(o_ret, ones_bd)
            var = _seg_mean(xc * xc, ones_bd)
            gate = proj_ref[rows, C_RET + 768:C_RET + 1024]
            y_ref[rows, 768:1024] = (xc * lax.rsqrt(var + EPS) * retn_ref[...] * gate).astype(y_ref.dtype)

        bb = b_scr[rows, 512:768]
        q = proj_ref[rows, C_HG:C_HG + 256]
        v = proj_ref[rows, C_HG + 256:C_HG + 512].astype(BF16)
        key = proj_ref[rows, key_col:key_col + 256]
        bend = b_scr[pl.ds(r0 + end_row, 1), 512:768]

        def brow(r):
            return b_scr[pl.ds(r0 + r, 1), 512:768]

        zero_row = jnp.zeros((1, 256), F32)
        if reverse:
            rs = [brow(SUB * i + SUB) if i < NSUB - 1 else zero_row for i in range(NSUB)]
            re = [brow(SUB * j) for j in range(NSUB)]
        else:
            rs = [brow(SUB * i - 1) if i > 0 else zero_row for i in range(NSUB)]
            re = [brow(SUB * j + SUB - 1) for j in range(NSUB)]
        qh = [q[SUB * i:SUB * (i + 1)] * jnp.exp(bb[SUB * i:SUB * (i + 1)] - rs[i]) for i in range(NSUB)]
        kh = jnp.concatenate(
            [key[SUB * j:SUB * (j + 1)] * jnp.exp(re[j] - bb[SUB * j:SUB * (j + 1)]) for j in range(NSUB)], axis=0)
        pairs = [(i, j) for i in range(NSUB) for j in range(NSUB) if (j >= i if reverse else j <= i)]
        lhs = jnp.concatenate(
            [qh[i] * jnp.exp(jnp.minimum(rs[i] - re[j], EXP_CLAMP)) for (i, j) in pairs], axis=0).astype(BF16)
        kbd = jnp.where(bd, _tile_rows(kh.astype(BF16), 4), jnp.zeros((), BF16))
        prod = _dot_nt(lhs, kbd)
        a_tiles = []
        for i in range(NSUB):
            a_i = jnp.zeros((SUB, 256), F32)
            for p, (pi, pj) in enumerate(pairs):
                if pi == i:
                    a_i = jnp.where((sl8 >> 3) == pj, prod[SUB * p:SUB * (p + 1)], a_i)
            a_tiles.append(a_i)
        a_hg = jnp.where(causal256, jnp.concatenate(a_tiles, axis=0), 0.0).astype(BF16)
        vbd = jnp.where(bd, _tile_rows(v, 4), jnp.zeros((), BF16))
        o_hg = _dot(a_hg, vbd) + _dot_nt((q * jnp.exp(bb)).astype(BF16), s_hg[...].astype(BF16))
        u = _dot_tn(v, (key * jnp.exp(bend - bb)).astype(BF16))
        s_hg[...] = s_hg[...] * jnp.exp(bend) + jnp.where(bd, u, 0.0)
        if reverse:
            y_ref[rows, 0:256] = o_hg
        else:
            o_hg = o_hg + obwd_ref[rows, 0:256]
            ms = _seg_mean(o_hg * o_hg, ones_bd)
            gate = proj_ref[rows, C_HG + 1536:C_HG + 1792]
            y_ref[rows, 0:256] = (o_hg * lax.rsqrt(ms + EPS) * hgn_ref[...] * gate).astype(y_ref.dtype)
        return carry

    lax.fori_loop(0, NCH, chunk_body, 0)

    sssd_out_ref[0] = s_ssd[0:64, :] + s_ssd[64:128, :]
    sret_out_ref[0] = s_ret[0:64, :] + s_ret[64:128, :] + s_ret[128:192, :] + s_ret[192:256, :]
    shg_out_ref[0] = s_hg[0:64, :] + s_hg[64:128, :] + s_hg[128:192, :] + s_hg[192:256, :]


def _block_table(seq_lens, reverse):
    rows = []
    blk0 = 0
    max_blocks = max(t // TB for t in seq_lens)
    for s, t in enumerate(seq_lens):
        nb = t // TB
        for k in range(nb):
            pos = (nb - 1 - k) if reverse else k
            rows.append([blk0 + pos, s, int(k == 0), int(pos == 0), int(pos == nb - 1), pos])
        blk0 += nb
    return np.asarray(rows, np.int32), blk0, max_blocks


def _tri_matrix(reverse):
    t = np.arange(TB)
    m = (t[None, :] >= t[:, None]) if reverse else (t[None, :] <= t[:, None])
    m = m & ((t[None, :] // CHUNK) == (t[:, None] // CHUNK))
    return jnp.asarray(m, BF16)


def _retention_constants(reverse):
    offset = RET_BWD_OFFSET if reverse else 0.0
    ld = np.log1p(-(2.0 ** (-5.0 - offset - np.arange(RET_HEADS, dtype=np.float64))))
    ld = np.repeat(ld, HEAD_DIM)[None, :]
    t = np.arange(CHUNK, dtype=np.float64)[:, None]
    s = (np.arange(256) % HEAD_DIM)[None, :].astype(np.float64)
    if reverse:
        b = (CHUNK - t) * ld
        dmat = np.where(s >= t, np.exp((s - t) * ld), 0.0)
    else:
        b = (t + 1.0) * ld
        dmat = np.where(s <= t, np.exp((t - s) * ld), 0.0)
    bend = CHUNK * ld
    f = lambda a: jnp.asarray(a, F32)
    return f(dmat), f(np.exp(b)), f(np.exp(bend - b)), f(np.exp(bend))


def _rope_tables(seq_lens, use_rope):
    n_pos = max(t for t, r in zip(seq_lens, use_rope) if r) if any(use_rope) else 0
    n_freq = HEAD_DIM // 4
    tok = np.arange(n_pos)
    rowp = (tok // GRID_W).astype(np.float32)
    colp = (tok % GRID_W).astype(np.float32)
    inv_freq = (ROPE_BASE ** (-np.arange(n_freq, dtype=np.float32) / n_freq)).astype(np.float32)
    ang = np.concatenate([rowp[:, None] * inv_freq, colp[:, None] * inv_freq], axis=-1)
    cos, sin = jnp.cos(jnp.asarray(ang)), jnp.sin(jnp.asarray(ang))
    cos_h = jnp.concatenate([cos, cos], axis=-1)
    sin_h = jnp.concatenate([-sin, sin], axis=-1)
    cos_t = jnp.concatenate([jnp.tile(cos_h, (1, RET_HEADS)), jnp.ones((TB, 256), F32)], axis=0)
    sin_t = jnp.concatenate([jnp.tile(sin_h, (1, RET_HEADS)), jnp.zeros((TB, 256), F32)], axis=0)
    return cos_t, sin_t, n_pos // TB


def mixer_direction(proj, seq_lens, use_rope, params, states, obwd, reverse):
    n_tok = proj.shape[0]
    n_seq = len(seq_lens)
    n_state = states[0].shape[0]
    tbl, n_blocks, _ = _block_table(seq_lens, reverse)
    cos_t, sin_t, ident_blk = _rope_tables(seq_lens, use_rope)
    rope_blk = np.asarray([tbl[i, 5] if use_rope[tbl[i, 1]] else ident_blk for i in range(n_blocks)], np.int32)
    tbl = np.concatenate([tbl[:, :5], rope_blk[:, None],
                          np.maximum(tbl[:, 0:1] * (TB // 8) - 1, 0),
                          np.minimum((tbl[:, 0:1] + 1) * (TB // 8), n_tok // 8 - 1),
                          np.minimum(tbl[:, 1:2], n_state - 1),
                          (tbl[:, 1:2] < n_state).astype(np.int32)], axis=1).astype(np.int32)
    d = 1 if reverse else 0
    retd, reteq, retek, reteend = _retention_constants(reverse)
    rep = lambda a: jnp.repeat(a.astype(F32), HEAD_DIM)[None, :]
    expand = np.zeros((128, 512), np.float32)
    for h in range(SSD_HEADS):
        expand[8 * d + h, HEAD_DIM * h:HEAD_DIM * (h + 1)] = 1.0
    small = [
        _tri_matrix(reverse), jnp.asarray(expand, BF16),
        params["conv_w"].astype(F32), params["conv_b"].astype(F32)[None, :],
        rep(params["d_skip"]),
        params["hg_norm_g"].astype(F32)[None, :], params["ssd_norm_g"].astype(F32)[None, :],
        params["ret_norm_g"].astype(F32)[None, :],
        retd, reteq, retek, reteend,
    ]
    full = lambda a: pl.BlockSpec(a.shape, lambda i, t: (0,) * a.ndim)
    in_specs = [
        pl.BlockSpec((TB, P_COLS), lambda i, t: (t[i, 0], 0)),
        pl.BlockSpec((8, 768), lambda i, t: (t[i, 6], 0)),
        pl.BlockSpec((8, 768), lambda i, t: (t[i, 7], 0)),
        pl.BlockSpec((TB, 256), lambda i, t: (t[i, 5], 0)),
        pl.BlockSpec((TB, 256), lambda i, t: (t[i, 5], 0)),
    ] + [full(a) for a in small] + [
        pl.BlockSpec((1, 64, 512), lambda i, t: (t[i, 8], 0, 0)),
        pl.BlockSpec((1, 64, 256), lambda i, t: (t[i, 8], 0, 0)),
        pl.BlockSpec((1, 64, 256), lambda i, t: (t[i, 8], 0, 0)),
    ]
    args = [proj, proj, proj, cos_t, sin_t] + small + list(states)
    if not reverse:
        in_specs.append(pl.BlockSpec((TB, 1024), lambda i, t: (t[i, 0], 0)))
        args.append(obwd)
    out_dtype = F32 if reverse else BF16
    out_shape = [
        jax.ShapeDtypeStruct((n_tok, 1024), out_dtype),
        jax.ShapeDtypeStruct((n_seq, 64, 512), F32),
        jax.ShapeDtypeStruct((n_seq, 64, 256), F32),
        jax.ShapeDtypeStruct((n_seq, 64, 256), F32),
    ]
    out_specs = [
        pl.BlockSpec((TB, 1024), lambda i, t: (t[i, 0], 0)),
        pl.BlockSpec((1, 64, 512), lambda i, t: (t[i, 1], 0, 0)),
        pl.BlockSpec((1, 64, 256), lambda i, t: (t[i, 1], 0, 0)),
        pl.BlockSpec((1, 64, 256), lambda i, t: (t[i, 1], 0, 0)),
    ]
    scratch = [
        pltpu.VMEM((TB, 1280), F32),
        pltpu.VMEM((128, 512), F32),
        pltpu.VMEM((256, 256), F32),
        pltpu.VMEM((256, 256), F32),
    ]
    return pl.pallas_call(
        functools.partial(_mixer_kernel, reverse=reverse),
        grid_spec=pltpu.PrefetchScalarGridSpec(
            num_scalar_prefetch=1, grid=(n_blocks,), in_specs=in_specs, out_specs=out_specs,
            scratch_shapes=scratch),
        out_shape=out_shape,
        compiler_params=pltpu.CompilerParams(dimension_semantics=("arbitrary",), vmem_limit_bytes=48 * 1024 * 1024),
        name="mixer_bwd" if reverse else "mixer_fwd",
    )(jnp.asarray(tbl), *args)


ADA_TN = 1024


def _ada_kernel(c_ref, w_ref, b_ref, o_ref):
    s = _silu(c_ref[...]).astype(BF16)
    o_ref[0] = _dot(s, w_ref[0].astype(BF16)) + b_ref[0]


def ada_modulation(cond, w_ada, b_ada):
    n_c = cond.shape[0]
    n_l, d, n_out = w_ada.shape
    return pl.pallas_call(
        _ada_kernel,
        grid=(n_l, n_out // ADA_TN),
        in_specs=[pl.BlockSpec((n_c, d), lambda l, j: (0, 0)),
                  pl.BlockSpec((1, d, ADA_TN), lambda l, j: (l, 0, j)),
                  pl.BlockSpec((1, 1, ADA_TN), lambda l, j: (l, 0, j))],
        out_specs=pl.BlockSpec((1, n_c, ADA_TN), lambda l, j: (l, 0, j)),
        out_shape=jax.ShapeDtypeStruct((n_l, n_c, n_out), F32),
        name="ada_mod",
    )(cond, w_ada, b_ada.reshape(n_l, 1, n_out))


TM = 256


def _rms(x):
    return x * lax.rsqrt(jnp.mean(x * x, axis=-1, keepdims=True) + EPS)


def _load_tokens(ref, n_tok):
    return jnp.concatenate([ref[pl.ds(c, n_tok, stride=8), :] for c in range(8)], axis=1)


def _store_tokens(ref, val, n_tok):
    for c in range(8):
        ref[pl.ds(c, n_tok, stride=8), :] = val[:, 128 * c:128 * (c + 1)]


def _inproj_kernel(crow_ref, x_ref, mod_ref, g_ref, w_ref, dtb_ref, alog_ref, lb_ref, o_ref):
    r = crow_ref[pl.program_id(0)]
    sh = mod_ref[pl.ds(r, 1), 0:1024]
    sc = mod_ref[pl.ds(r, 1), 1024:2048]
    h = _rms(_load_tokens(x_ref, TM)) * g_ref[...] * (1.0 + sc) + sh
    raw = _dot(h.astype(BF16), w_ref[...])

    o_ref[:, C_XBC:C_XBC + 768] = raw[:, R_XBC:R_XBC + 768]
    o_ref[:, C_Z:C_Z + 512] = _silu(raw[:, R_Z:R_Z + 512])
    dt = _softplus(raw[:, R_DT:R_DT + 128] + dtb_ref[...])
    o_ref[:, C_DT:C_DT + 128] = dt
    o_ref[:, C_LA:C_LA + 128] = -dt * jnp.exp(alog_ref[...])
    o_ref[:, C_HG:C_HG + 256] = _silu(raw[:, R_HG:R_HG + 256]) * (HEAD_DIM ** -0.5)
    o_ref[:, C_HG + 256:C_HG + 512] = raw[:, R_HG + 256:R_HG + 512]
    for dirn in range(2):
        lb = lb_ref[dirn:dirn + 1, :]
        fr = raw[:, R_HG + 512 + 256 * dirn:R_HG + 768 + 256 * dirn]
        o_ref[:, C_HG + 512 + 256 * dirn:C_HG + 768 + 256 * dirn] = jnp.log(lb + (1.0 - lb) * _sigmoid(fr))
        o_ref[:, C_HG + 1024 + 256 * dirn:C_HG + 1280 + 256 * dirn] = (1.0 - lb) * _sigmoid(-fr)
    o_ref[:, C_HG + 1536:C_HG + 1792] = _silu(raw[:, R_HG + 1024:R_HG + 1280])
    o_ref[:, C_RET:C_RET + 256] = raw[:, R_RET:R_RET + 256]
    o_ref[:, C_RET + 256:C_RET + 512] = raw[:, R_RET + 256:R_RET + 512] * (HEAD_DIM ** -0.5)
    o_ref[:, C_RET + 512:C_RET + 768] = raw[:, R_RET + 512:R_RET + 768]
    o_ref[:, C_RET + 768:C_RET + 1024] = _silu(raw[:, R_RET + 768:R_RET + 1024])


def in_projection(x8, crow, mod, gain, w, dt_bias, a_log, lb):
    n_tok, d = x8.shape[0] // 8, D_MODEL
    compact = lambda a: jnp.zeros((1, 128), F32).at[0, :2 * SSD_HEADS].set(a.astype(F32).reshape(-1))
    return pl.pallas_call(
        _inproj_kernel,
        grid_spec=pltpu.PrefetchScalarGridSpec(
            num_scalar_prefetch=1, grid=(n_tok // TM,),
            in_specs=[pl.BlockSpec((TM * 8, 128), lambda i, c: (i, 0)),
                      pl.BlockSpec(mod.shape, lambda i, c: (0, 0)),
                      pl.BlockSpec((1, d), lambda i, c: (0, 0)),
                      pl.BlockSpec((d, R_COLS), lambda i, c: (0, 0)),
                      pl.BlockSpec((1, 128), lambda i, c: (0, 0)),
                      pl.BlockSpec((1, 128), lambda i, c: (0, 0)),
                      pl.BlockSpec((2, HG_W), lambda i, c: (0, 0))],
            out_specs=pl.BlockSpec((TM, P_COLS), lambda i, c: (i, 0))),
        out_shape=jax.ShapeDtypeStruct((n_tok, P_COLS), F32),
        compiler_params=pltpu.CompilerParams(dimension_semantics=("arbitrary",), vmem_limit_bytes=48 * 1024 * 1024),
        name="in_proj",
    )(crow, x8, mod, gain[None, :], w, compact(dt_bias), compact(a_log), lb.astype(F32))


def _outproj_kernel(crow_ref, y_ref, x_ref, mod_ref, g_ref, wo_ref, wr_ref, x1_ref, aff_ref):
    r = crow_ref[pl.program_id(0)]
    g1 = mod_ref[pl.ds(r, 1), 2048:3072]
    sh = mod_ref[pl.ds(r, 1), 3072:4096]
    sc = mod_ref[pl.ds(r, 1), 4096:5120]
    x1 = _load_tokens(x_ref, TM) + g1 * _dot(y_ref[...], wo_ref[...])
    _store_tokens(x1_ref, x1, TM)
    h = _rms(x1) * g_ref[...] * (1.0 + sc) + sh
    lt = _dot_nt(wr_ref[...], h.astype(BF16))
    e = jnp.exp(lt - jnp.max(lt, axis=0, keepdims=True))
    aff_ref[...] = e / jnp.sum(e, axis=0, keepdims=True)


def out_projection(y, x8, crow, mod, gain, w_out, w_router):
    n_tok, d = x8.shape[0] // 8, D_MODEL
    wr = w_router.astype(BF16).T
    return pl.pallas_call(
        _outproj_kernel,
        grid_spec=pltpu.PrefetchScalarGridSpec(
            num_scalar_prefetch=1, grid=(n_tok // TM,),
            in_specs=[pl.BlockSpec((TM, d), lambda i, c: (i, 0)),
                      pl.BlockSpec((TM * 8, 128), lambda i, c: (i, 0)),
                      pl.BlockSpec(mod.shape, lambda i, c: (0, 0)),
                      pl.BlockSpec((1, d), lambda i, c: (0, 0)),
                      pl.BlockSpec((d, d), lambda i, c: (0, 0)),
                      pl.BlockSpec((N_EXPERTS, d), lambda i, c: (0, 0))],
            out_specs=[pl.BlockSpec((TM * 8, 128), lambda i, c: (i, 0)),
                       pl.BlockSpec((N_EXPERTS, TM), lambda i, c: (0, i))]),
        out_shape=[jax.ShapeDtypeStruct((n_tok * 8, 128), F32), jax.ShapeDtypeStruct((N_EXPERTS, n_tok), F32)],
        compiler_params=pltpu.CompilerParams(dimension_semantics=("arbitrary",)),
        name="out_proj",
    )(crow, y, x8, mod, gain[None, :], w_out, wr)


TK_BLK = 256
TK_JB = 64


def _topk_kernel(aff_ref, tri_ref, idx_ref, gate_ref, csum_scr, sel_scr, ti_scr, tg_scr, *, n, cap, cpad, npair):
    a = aff_ref[...]

    def count_ge(t):
        return jnp.sum((a >= t).astype(jnp.int32), axis=1, keepdims=True)

    def search(it, lo):
        cand = lo | (1 << (30 - it))
        return jnp.where(count_ge(pltpu.bitcast(cand, F32)) >= cap, cand, lo)

    lo_bits = lax.fori_loop(0, 31, search, jnp.zeros((npair, 1), jnp.int32))

    def refine(it, lh):
        lo, hi = lh
        mid = 0.5 * (lo + hi)
        ok = count_ge(mid) >= cap
        return jnp.where(ok, mid, lo), jnp.where(ok, hi, mid)

    lo_v, hi_v = lax.fori_loop(0, 32, refine, (pltpu.bitcast(lo_bits, F32), pltpu.bitcast(lo_bits + 1, F32)))
    gt = a >= hi_v
    eq = (a >= lo_v) & (a < hi_v)
    need = (cap - jnp.sum(gt.astype(jnp.int32), axis=1, keepdims=True)).astype(F32)

    tri = tri_ref[...]

    def prefix(mask):
        carry = jnp.zeros((npair, 1), F32)
        parts = []
        for kb in range(n // TK_BLK):
            blk = mask[:, kb * TK_BLK:(kb + 1) * TK_BLK].astype(BF16)
            loc = _dot(blk, tri) + carry
            parts.append(loc)
            carry = loc[:, TK_BLK - 1:TK_BLK]
        return jnp.concatenate(parts, axis=1)

    sel = gt | (eq & (prefix(eq) <= need))
    csum_scr[...] = prefix(sel)
    sel_scr[...] = jnp.where(sel, a, 0.0)
    lane = lax.broadcasted_iota(jnp.int32, (TK_JB, max(npair, 128)), 1)

    def per_expert(e, carry):
        cs = csum_scr[pl.ds(e, 1), :]
        av = sel_scr[pl.ds(e, 1), :]

        def per_block(jb, c2):
            j0 = pl.multiple_of(jb * TK_JB, TK_JB)
            j = (lax.broadcasted_iota(jnp.int32, (TK_JB, n), 0) + j0).astype(F32)
            idx_col = jnp.sum((cs <= j).astype(F32), axis=1, keepdims=True)
            gate_col = jnp.sum(jnp.where(cs == j + 1.0, av, 0.0), axis=1, keepdims=True)
            rows = pl.ds(j0, TK_JB)
            ti_scr[rows, :] = jnp.where(lane == e, idx_col, ti_scr[rows, :])
            tg_scr[rows, :] = jnp.where(lane == e, gate_col, tg_scr[rows, :])
            return c2

        return lax.fori_loop(0, cpad // TK_JB, per_block, carry)

    lax.fori_loop(0, npair, per_expert, 0)
    idx_ref[...] = ti_scr[...].T[0:npair, :].astype(jnp.int32)
    gate_ref[...] = tg_scr[...].T[0:npair, :]


def expert_choice_topk(aff_t, tok0, n_rows, n, rows_per_step):
    cap = EC_CAPACITY_FACTOR * n // N_EXPERTS
    cpad = max(cap, 128)
    npair = rows_per_step * N_EXPERTS
    plane = max(npair, 128)
    pairs = aff_t[:, tok0:tok0 + n_rows * n].reshape(N_EXPERTS, n_rows, n).transpose(1, 0, 2).reshape(n_rows * N_EXPERTS, n)
    t = np.arange(TK_BLK)
    tri = jnp.asarray(t[:, None] <= t[None, :], BF16)
    idx, gate = pl.pallas_call(
        functools.partial(_topk_kernel, n=n, cap=cap, cpad=cpad, npair=npair),
        grid=(n_rows // rows_per_step,),
        in_specs=[pl.BlockSpec((npair, n), lambda r: (r, 0)),
                  pl.BlockSpec((TK_BLK, TK_BLK), lambda r: (0, 0))],
        out_specs=[pl.BlockSpec((npair, cpad), lambda r: (r, 0)),
                   pl.BlockSpec((npair, cpad), lambda r: (r, 0))],
        out_shape=[jax.ShapeDtypeStruct((n_rows * N_EXPERTS, cpad), jnp.int32),
                   jax.ShapeDtypeStruct((n_rows * N_EXPERTS, cpad), F32)],
        scratch_shapes=[pltpu.VMEM((npair, n), F32), pltpu.VMEM((npair, n), F32),
                        pltpu.VMEM((cpad, plane), F32), pltpu.VMEM((cpad, plane), F32)],
        compiler_params=pltpu.CompilerParams(dimension_semantics=("arbitrary",)),
        name=f"topk_{n}",
    )(pairs, tri)
    return (idx[:, :cap].reshape(n_rows, N_EXPERTS, cap), gate[:, :cap].reshape(n_rows, N_EXPERTS, cap))


MOE_NT = 4096
MOE_C = 512
MOE_GROUP = 8


def _moe_kernel(crow_ref, idx_ref, gate_ref, x_ref, mod_ref, g_ref, wg_ref, wu_ref, wd_ref, o_ref, xs_scr, ys_scr):
    grp = pl.program_id(0)
    e = pl.program_id(1)
    r = crow_ref[grp]
    base = (grp * N_EXPERTS + e) * MOE_C
    x8 = x_ref.at[0]
    o8 = o_ref.at[0]

    def tile(i):
        return pl.ds(pl.multiple_of(i * 8, 8), 8)

    @pl.when(e == 0)
    def _():
        o_ref[...] = x_ref[...]

    def gather(j, c):
        xs_scr[tile(j), :] = x8[tile(idx_ref[base + j]), :]
        return c

    lax.fori_loop(0, MOE_C, gather, 0, unroll=8)

    sh = mod_ref[pl.ds(r, 1), 3072:4096]
    sc = mod_ref[pl.ds(r, 1), 4096:5120]
    g2 = mod_ref[pl.ds(r, 1), 5120:6144]
    h = (_rms(_load_tokens(xs_scr, MOE_C)) * g_ref[...] * (1.0 + sc) + sh).astype(BF16)
    hid = _silu(_dot(h, wg_ref[0])) * _dot(h, wu_ref[0])
    gate_col = jnp.broadcast_to(gate_ref[0], (128, MOE_C)).T[:, 0:1]
    _store_tokens(ys_scr, _dot(hid.astype(BF16), wd_ref[0]) * g2 * gate_col, MOE_C)

    def scatter(grp_j, c):
        j0 = grp_j * MOE_GROUP
        toks = [idx_ref[base + j0 + k] for k in range(MOE_GROUP)]
        vals = [o8[tile(toks[k]), :] + ys_scr[tile(j0 + k), :] for k in range(MOE_GROUP)]
        for k in range(MOE_GROUP):
            o8[tile(toks[k]), :] = vals[k]
        return c

    lax.fori_loop(0, MOE_C // MOE_GROUP, scatter, 0)


def moe_ffn(x8, idx, gate, crow_grp, mod, gain, w_gate, w_up, w_down):
    n_tok, d = x8.shape[0] // 8, D_MODEL
    n_grp = n_tok // MOE_NT
    xg = x8.reshape(n_grp, MOE_NT * 8, 128)
    out = pl.pallas_call(
        _moe_kernel,
        grid_spec=pltpu.PrefetchScalarGridSpec(
            num_scalar_prefetch=2, grid=(n_grp, N_EXPERTS),
            in_specs=[pl.BlockSpec((1, 1, MOE_C), lambda g, e, c, ix: (g * N_EXPERTS + e, 0, 0)),
                      pl.BlockSpec((1, MOE_NT * 8, 128), lambda g, e, c, ix: (g, 0, 0), pipeline_mode=pl.Buffered(1)),
                      pl.BlockSpec(mod.shape, lambda g, e, c, ix: (0, 0)),
                      pl.BlockSpec((1, d), lambda g, e, c, ix: (0, 0)),
                      pl.BlockSpec((1, d, D_EXPERT), lambda g, e, c, ix: (e, 0, 0)),
                      pl.BlockSpec((1, d, D_EXPERT), lambda g, e, c, ix: (e, 0, 0)),
                      pl.BlockSpec((1, D_EXPERT, d), lambda g, e, c, ix: (e, 0, 0))],
            out_specs=pl.BlockSpec((1, MOE_NT * 8, 128), lambda g, e, c, ix: (g, 0, 0), pipeline_mode=pl.Buffered(1)),
            scratch_shapes=[pltpu.VMEM((MOE_C * 8, 128), F32), pltpu.VMEM((MOE_C * 8, 128), F32)]),
        out_shape=jax.ShapeDtypeStruct((n_grp, MOE_NT * 8, 128), F32),
        compiler_params=pltpu.CompilerParams(dimension_semantics=("arbitrary", "arbitrary"),
                                             vmem_limit_bytes=56 * 1024 * 1024),
        name="moe_ffn",
    )(crow_grp, idx, gate, xg, mod, gain[None, :], w_gate, w_up, w_down)
    return out.reshape(n_tok * 8, 128)


def _final_kernel(x_ref, g_ref, o_ref):
    o_ref[...] = _rms(_load_tokens(x_ref, TM)) * g_ref[...]


def final_norm(x8, gain, tok0, n_out):
    blk0 = tok0 // TM
    return pl.pallas_call(
        _final_kernel,
        grid=(n_out // TM,),
        in_specs=[pl.BlockSpec((TM * 8, 128), lambda i: (blk0 + i, 0)),
                  pl.BlockSpec((1, D_MODEL), lambda i: (0, 0))],
        out_specs=pl.BlockSpec((TM, D_MODEL), lambda i: (i, 0)),
        out_shape=jax.ShapeDtypeStruct((n_out, D_MODEL), F32),
        name="final_norm",
    )(x8, gain[None, :])


def _ssd_to_k(s):
    return s.transpose(0, 2, 1, 3).reshape(s.shape[0], 64, 512)


def _ssd_from_k(m):
    return m.reshape(m.shape[0], 64, 8, 64).transpose(0, 2, 1, 3)


def _ret_to_k(s):
    return s.transpose(0, 2, 1, 3).reshape(s.shape[0], 64, 256)


def _ret_from_k(m):
    return m.reshape(m.shape[0], 64, 4, 64).transpose(0, 2, 1, 3)


def _hg_to_k(s):
    return s.transpose(0, 3, 1, 2).reshape(s.shape[0], 64, 256)


def _hg_from_k(m):
    return m.reshape(m.shape[0], 64, 4, 64).transpose(0, 2, 3, 1)


def _arrange_w_in(w):
    hg = w[:, 0:1280]
    z = w[:, 1280:1792]
    xbc = w[:, 1792:2560]
    dt = jnp.pad(w[:, 2560:2576], ((0, 0), (0, 128 - 2 * SSD_HEADS)))
    ret = w[:, 2576:3600]
    return jnp.concatenate([xbc, z, dt, hg, ret], axis=1).astype(BF16)


def kernel(x_prompt, x_sample, state_hgrn, state_ssd, state_ret, c, c_ctx, norm1_g, norm2_g, final_g,
           w_ada, b_ada, w_in, w_out, hg_lb, hg_norm_g, ssd_conv_w, ssd_conv_b, ssd_dt_bias, ssd_a_log,
           ssd_d, ssd_norm_g, ret_norm_g, w_router, w_gate, w_up, w_down):
    n_ctx, t_ctx, d = x_prompt.shape
    n_smp, t_smp, _ = x_sample.shape
    n_tok_s = n_smp * t_smp
    n_tok = n_tok_s + n_ctx * t_ctx
    seq_lens = [t_smp] * n_smp + [t_ctx] * n_ctx
    use_rope = [True] * n_smp + [False] * n_ctx

    x = jnp.concatenate([x_sample.reshape(n_tok_s * 8, 128), x_prompt.reshape(n_ctx * t_ctx * 8, 128)], axis=0)
    cond = jnp.zeros((16, d), F32).at[0].set(c_ctx).at[1:1 + n_smp].set(c)
    crow_tm = jnp.asarray([i * TM // t_smp + 1 if i * TM < n_tok_s else 0 for i in range(n_tok // TM)], jnp.int32)
    crow_grp = jnp.asarray([g + 1 if g < n_smp else 0 for g in range(n_tok // MOE_NT)], jnp.int32)

    lb_p = jax.nn.softmax(hg_lb.astype(F32), axis=0)
    lb_all = jnp.cumsum(lb_p, axis=0) - lb_p[0]
    mod_all = ada_modulation(cond, w_ada, b_ada)

    rows_per_grp = MOE_NT // t_ctx
    n_grp_ctx = n_ctx // rows_per_grp
    new_states = {"hg": [], "ssd": [], "ret": []}
    for l in range(DEPTH):
        mod = mod_all[l]
        proj = in_projection(x, crow_tm, mod, norm1_g[l], _arrange_w_in(w_in[l]), ssd_dt_bias[l], ssd_a_log[l],
                             lb_all[l])
        params = dict(conv_w=ssd_conv_w[l], conv_b=ssd_conv_b[l], d_skip=ssd_d[l], hg_norm_g=hg_norm_g[l],
                      ssd_norm_g=ssd_norm_g[l], ret_norm_g=ret_norm_g[l])
        outs = []
        for dirn in (1, 0):
            st = (_ssd_to_k(state_ssd[:, l, dirn]), _ret_to_k(state_ret[:, l, dirn]), _hg_to_k(state_hgrn[:, l, dirn]))
            outs.append(mixer_direction(proj, seq_lens, use_rope, params, st, outs[0][0] if outs else None, dirn == 1))
        (_, b_ssd, b_ret, b_hg), (y, f_ssd, f_ret, f_hg) = outs
        new_states["ssd"].append(jnp.stack([_ssd_from_k(f_ssd[n_smp:]), _ssd_from_k(b_ssd[n_smp:])], axis=1))
        new_states["ret"].append(jnp.stack([_ret_from_k(f_ret[n_smp:]), _ret_from_k(b_ret[n_smp:])], axis=1))
        new_states["hg"].append(jnp.stack([_hg_from_k(f_hg[n_smp:]), _hg_from_k(b_hg[n_smp:])], axis=1))

        x1, aff_t = out_projection(y, x, crow_tm, mod, norm2_g[l], w_out[l].astype(BF16), w_router[l])
        idx_s, gate_s = expert_choice_topk(aff_t, 0, n_smp, t_smp, 1)
        idx_c, gate_c = expert_choice_topk(aff_t, n_tok_s, n_ctx, t_ctx, rows_per_grp)
        cap_c = idx_c.shape[-1]
        idx_c = idx_c + (jnp.arange(n_ctx, dtype=jnp.int32) % rows_per_grp)[:, None, None] * t_ctx
        regroup = lambda a: a.reshape(n_grp_ctx, rows_per_grp, N_EXPERTS, cap_c).transpose(0, 2, 1, 3).reshape(
            n_grp_ctx, N_EXPERTS, rows_per_grp * cap_c)
        idx = jnp.concatenate([idx_s, regroup(idx_c)], axis=0).reshape(-1)
        gate = jnp.concatenate([gate_s, regroup(gate_c)], axis=0).reshape(-1, 1, MOE_C)
        x = moe_ffn(x1, idx, gate, crow_grp, mod, norm2_g[l], w_gate[l].astype(BF16),
                    w_up[l].astype(BF16), w_down[l].astype(BF16))

    y_sample = final_norm(x, final_g, 0, n_tok_s).reshape(n_smp, t_smp, d)
    y_prompt = final_norm(x, final_g, n_tok_s, n_ctx * t_ctx).reshape(n_ctx, t_ctx, d)
    return (y_prompt, y_sample, jnp.stack(new_states["hg"], axis=1), jnp.stack(new_states["ssd"], axis=1),
            jnp.stack(new_states["ret"], axis=1))
```

```python
import functools
import math

import jax
import jax.numpy as jnp
import numpy as np
from jax import lax
from jax.experimental import pallas as pl
from jax.experimental.pallas import tpu as pltpu

D_MODEL = 1024
DEPTH = 2
GRID_W = 64
CHUNK = 64
EPS = 1e-6
HEAD_DIM = 64
HG_W = 256
SSD_W = 512
RET_W = 256
HG_HEADS = 4
SSD_HEADS = 8
SSD_N = 64
SSD_GROUPS = 2
SSD_CONV_CH = 768
RET_HEADS = 4
RET_BWD_OFFSET = 0.5
ROPE_BASE = 10000.0
N_EXPERTS = 16
EC_CAPACITY_FACTOR = 2
D_EXPERT = 512

R_XBC = 0
R_Z = 768
R_DT = 1280
R_HG = 1408
R_RET = 2688
R_COLS = 3712

C_XBC = 0
C_Z = 768
C_DT = 1280
C_LA = 1408
C_HG = 1536
C_RET = 3328
P_COLS = 4352

TB = 256
NCH = TB // CHUNK
SUB = 8
NSUB = CHUNK // SUB
EXP_CLAMP = 80.0

F32 = jnp.float32
BF16 = jnp.bfloat16


def _sigmoid(x):
    return 1.0 / (1.0 + jnp.exp(-x))


def _silu(x):
    return x * (0.5 + 0.5 * jnp.tanh(0.5 * x))


def _softplus(x):
    return jnp.maximum(x, 0.0) + jnp.log1p(jnp.exp(-jnp.abs(x)))


def _dot(a, b):
    return jnp.dot(a, b, preferred_element_type=F32)


def _dot_nt(a, b):
    return lax.dot_general(a, b, (((1,), (1,)), ((), ())), preferred_element_type=F32)


def _dot_tn(a, b):
    return lax.dot_general(a, b, (((0,), (0,)), ((), ())), preferred_element_type=F32)


def _split3(x):
    hi = x.astype(BF16)
    r = x - hi.astype(F32)
    mid = r.astype(BF16)
    lo = (r - mid.astype(F32)).astype(BF16)
    return hi, mid, lo


def _seg_mean(x, ones_bd):
    hi = x.astype(BF16)
    lo = (x - hi.astype(F32)).astype(BF16)
    return (_dot(hi, ones_bd) + _dot(lo, ones_bd)) * (1.0 / HEAD_DIM)


def _tile_rows(x, n):
    return jnp.concatenate([x] * n, axis=0)


def _mixer_kernel(tbl_ref, proj_ref, prev_ref, next_ref, own_ref, tri_ref, expand_ref,
                  convw_ref, dskip_ref,
                  hgn_ref, ssdn_ref, retn_ref, retd_ref, reteq_ref, retek_ref, reteend_ref,
                  sssd_in_ref, sret_in_ref, shg_in_ref, *rest, reverse):
    if reverse:
        (y_ref, sssd_out_ref, sret_out_ref, shg_out_ref, b_scr, s_ssd, s_ret, s_hg) = rest
        obwd_ref = None
    else:
        (obwd_ref, y_ref, sssd_out_ref, sret_out_ref, shg_out_ref, b_scr, s_ssd, s_ret, s_hg) = rest

    step = pl.program_id(0)
    is_first = tbl_ref[step, 2]
    keep_prev = (1 - tbl_ref[step, 3]).astype(F32)
    keep_next = (1 - tbl_ref[step, 4]).astype(F32)

    r256 = lax.broadcasted_iota(jnp.int32, (256, 256), 0)
    c256 = lax.broadcasted_iota(jnp.int32, (256, 256), 1)
    bd = (r256 >> 6) == (c256 >> 6)
    gmask_s = (lax.broadcasted_iota(jnp.int32, (128, 512), 0) >> 6) == (lax.broadcasted_iota(jnp.int32, (128, 512), 1) >> 8)

    @pl.when(is_first == 1)
    def _():
        has = tbl_ref[step, 9].astype(F32)
        s_ssd[...] = jnp.where(gmask_s, _tile_rows(sssd_in_ref[0], 2), 0.0) * has
        s_ret[...] = jnp.where(bd, _tile_rows(sret_in_ref[0], 4), 0.0) * has
        s_hg[...] = jnp.where(bd, _tile_rows(shg_in_ref[0], 4), 0.0) * has

    lf_col = C_HG + (768 if reverse else 512)
    key_col = C_HG + (1280 if reverse else 1024)

    xbc_first = _silu(own_ref[2:3, :] + convw_ref[0:1, :] * prev_ref[1:2, :] * keep_prev)
    xbc_last = _silu(own_ref[3:4, :] + convw_ref[2:3, :] * next_ref[0:1, :] * keep_next)

    def split2(a):
        hi = a.astype(BF16)
        return hi, (a - hi.astype(F32)).astype(BF16)

    la_hi, la_lo = split2(proj_ref[:, C_LA:C_LA + 128])
    lf_hi, lf_lo = split2(proj_ref[:, lf_col:lf_col + 256])
    cum = _dot(tri_ref[...], jnp.concatenate([la_hi, la_lo, lf_hi, lf_lo], axis=1))
    b_scr[:, 512:768] = cum[:, 256:512] + cum[:, 512:768]
    c_hi, c_lo = split2(cum[:, 0:128] + cum[:, 128:256])
    d_hi, d_lo = split2(proj_ref[:, C_DT:C_DT + 128])
    spread = _dot(jnp.concatenate([c_hi, c_lo, d_hi, d_lo], axis=0), expand_ref[...])
    b_scr[:, 0:512] = spread[0:TB] + spread[TB:2 * TB]
    b_scr[:, 768:1280] = spread[2 * TB:3 * TB] + spread[3 * TB:4 * TB]

    gmask_b = (lax.broadcasted_iota(jnp.int32, (512, 128), 0) >> 8) == (lax.broadcasted_iota(jnp.int32, (512, 128), 1) >> 6)
    t512 = lax.broadcasted_iota(jnp.int32, (CHUNK, 512), 0)
    s512 = lax.broadcasted_iota(jnp.int32, (CHUNK, 512), 1) & 63
    t256 = lax.broadcasted_iota(jnp.int32, (CHUNK, 256), 0)
    s256 = lax.broadcasted_iota(jnp.int32, (CHUNK, 256), 1) & 63
    if reverse:
        causal512, causal256 = s512 >= t512, s256 >= t256
    else:
        causal512, causal256 = s512 <= t512, s256 <= t256
    diag512 = s512 == t512
    sl8 = lax.broadcasted_iota(jnp.int32, (SUB, 256), 1) & 63
    row768 = lax.broadcasted_iota(jnp.int32, (CHUNK, 768), 0)
    ones_bd = bd.astype(BF16)
    end_row = 0 if reverse else CHUNK - 1

    for k in range(NCH):
        c = (NCH - 1 - k) if reverse else k
        r0 = c * CHUNK
        rows = pl.ds(r0, CHUNK)

        xbc = proj_ref[rows, C_XBC:C_XBC + 768]
        if c == 0:
            xbc = jnp.where(row768 == 0, xbc_first, xbc)
        if c == NCH - 1:
            xbc = jnp.where(row768 == CHUNK - 1, xbc_last, xbc)
        bb = b_scr[rows, 0:512]
        dtc = b_scr[rows, 768:1280]
        xs = xbc[:, 0:512]
        bm = xbc[:, 512:640].astype(BF16)
        cm = xbc[:, 640:768].astype(BF16)
        xdt = xs * dtc
        bend = b_scr[pl.ds(r0 + end_row, 1), 0:512]
        rvec = jnp.sum(jnp.where(diag512, bb, 0.0), axis=0, keepdims=True)
        dmat = jnp.where(causal512, jnp.exp(jnp.minimum(bb - rvec, 0.0)), 0.0)
        rhs_b = jnp.where(gmask_b, _tile_rows(bm, 8), jnp.zeros((), BF16))
        a_ssd = (_dot_nt(cm, rhs_b) * dmat).astype(BF16)
        xdt_b = xdt.astype(BF16)
        o_parts = []
        for g in range(SSD_GROUPS):
            vbd = jnp.where(bd, _tile_rows(xdt_b[:, g * 256:(g + 1) * 256], 4), jnp.zeros((), BF16))
            o_parts.append(_dot(a_ssd[:, g * 256:(g + 1) * 256], vbd))
        o_ssd = jnp.concatenate(o_parts, axis=1) + jnp.exp(bb) * _dot(cm, s_ssd[...].astype(BF16))
        u = _dot_tn(bm, (xdt * jnp.exp(bend - bb)).astype(BF16))
        s_ssd[...] = s_ssd[...] * jnp.exp(bend) + jnp.where(gmask_s, u, 0.0)
        if reverse:
            y_ref[rows, 256:768] = o_ssd
        else:
            y = o_ssd + obwd_ref[rows, 256:768] + xs * dskip_ref[...]
            y = y * proj_ref[rows, C_Z:C_Z + 512]
            y = y * lax.rsqrt(jnp.mean(y * y, axis=-1, keepdims=True) + EPS) * ssdn_ref[...]
            y_ref[rows, 256:768] = y.astype(y_ref.dtype)

        q = proj_ref[rows, C_RET:C_RET + 256]
        kk = proj_ref[rows, C_RET + 256:C_RET + 512]
        v = proj_ref[rows, C_RET + 512:C_RET + 768].astype(BF16)
        kbd = jnp.where(bd, _tile_rows(kk.astype(BF16), 4), jnp.zeros((), BF16))
        a_ret = (_dot_nt(q.astype(BF16), kbd) * retd_ref[...]).astype(BF16)
        vbd = jnp.where(bd, _tile_rows(v, 4), jnp.zeros((), BF16))
        o_ret = _dot(a_ret, vbd) + _dot((q * reteq_ref[...]).astype(BF16), s_ret[...].astype(BF16))
        u = _dot_tn((kk * retek_ref[...]).astype(BF16), v)
        s_ret[...] = s_ret[...] * reteend_ref[...] + jnp.where(bd, u, 0.0)
        if reverse:
            y_ref[rows, 768:1024] = o_ret
        else:
            o_ret = o_ret + obwd_ref[rows, 768:1024]
            xc = o_ret - _seg_mean(o_ret, ones_bd)
            var = _seg_mean(xc * xc, ones_bd)
            gate = proj_ref[rows, C_RET + 768:C_RET + 1024]
            y_ref[rows, 768:1024] = (xc * lax.rsqrt(var + EPS) * retn_ref[...] * gate).astype(y_ref.dtype)

        bb = b_scr[rows, 512:768]
        q = proj_ref[rows, C_HG:C_HG + 256]
        v = proj_ref[rows, C_HG + 256:C_HG + 512].astype(BF16)
        key = proj_ref[rows, key_col:key_col + 256]
        bend = b_scr[pl.ds(r0 + end_row, 1), 512:768]

        def brow(r):
            return b_scr[pl.ds(r0 + r, 1), 512:768]

        zero_row = jnp.zeros((1, 256), F32)
        if reverse:
            rs = [brow(SUB * i + SUB) if i < NSUB - 1 else zero_row for i in range(NSUB)]
            re = [brow(SUB * j) for j in range(NSUB)]
        else:
            rs = [brow(SUB * i - 1) if i > 0 else zero_row for i in range(NSUB)]
            re = [brow(SUB * j + SUB - 1) for j in range(NSUB)]
        qh = [q[SUB * i:SUB * (i + 1)] * jnp.exp(bb[SUB * i:SUB * (i + 1)] - rs[i]) for i in range(NSUB)]
        kh = jnp.concatenate(
            [key[SUB * j:SUB * (j + 1)] * jnp.exp(re[j] - bb[SUB * j:SUB * (j + 1)]) for j in range(NSUB)], axis=0)
        pairs = [(i, j) for i in range(NSUB) for j in range(NSUB) if (j >= i if reverse else j <= i)]
        lhs = jnp.concatenate(
            [qh[i] * jnp.exp(jnp.minimum(rs[i] - re[j], EXP_CLAMP)) for (i, j) in pairs], axis=0).astype(BF16)
        kbd = jnp.where(bd, _tile_rows(kh.astype(BF16), 4), jnp.zeros((), BF16))
        prod = _dot_nt(lhs, kbd)
        a_tiles = []
        for i in range(NSUB):
            a_i = jnp.zeros((SUB, 256), F32)
            for p, (pi, pj) in enumerate(pairs):
                if pi == i:
                    a_i = jnp.where((sl8 >> 3) == pj, prod[SUB * p:SUB * (p + 1)], a_i)
            a_tiles.append(a_i)
        a_hg = jnp.where(causal256, jnp.concatenate(a_tiles, axis=0), 0.0).astype(BF16)
        vbd = jnp.where(bd, _tile_rows(v, 4), jnp.zeros((), BF16))
        o_hg = _dot(a_hg, vbd) + _dot_nt((q * jnp.exp(bb)).astype(BF16), s_hg[...].astype(BF16))
        u = _dot_tn(v, (key * jnp.exp(bend - bb)).astype(BF16))
        s_hg[...] = s_hg[...] * jnp.exp(bend) + jnp.where(bd, u, 0.0)
        if reverse:
            y_ref[rows, 0:256] = o_hg
        else:
            o_hg = o_hg + obwd_ref[rows, 0:256]
            ms = _seg_mean(o_hg * o_hg, ones_bd)
            gate = proj_ref[rows, C_HG + 1536:C_HG + 1792]
            y_ref[rows, 0:256] = (o_hg * lax.rsqrt(ms + EPS) * hgn_ref[...] * gate).astype(y_ref.dtype)

    sssd_out_ref[0] = s_ssd[0:64, :] + s_ssd[64:128, :]
    sret_out_ref[0] = s_ret[0:64, :] + s_ret[64:128, :] + s_ret[128:192, :] + s_ret[192:256, :]
    shg_out_ref[0] = s_hg[0:64, :] + s_hg[64:128, :] + s_hg[128:192, :] + s_hg[192:256, :]


def _block_table(seq_lens, reverse):
    rows = []
    blk0 = 0
    max_blocks = max(t // TB for t in seq_lens)
    for s, t in enumerate(seq_lens):
        nb = t // TB
        for k in range(nb):
            pos = (nb - 1 - k) if reverse else k
            rows.append([blk0 + pos, s, int(k == 0), int(pos == 0), int(pos == nb - 1), pos])
        blk0 += nb
    return np.asarray(rows, np.int32), blk0, max_blocks


def _tri_matrix(reverse):
    t = np.arange(TB)
    m = (t[None, :] >= t[:, None]) if reverse else (t[None, :] <= t[:, None])
    m = m & ((t[None, :] // CHUNK) == (t[:, None] // CHUNK))
    return jnp.asarray(m, BF16)


def _retention_constants(reverse):
    offset = RET_BWD_OFFSET if reverse else 0.0
    ld = np.log1p(-(2.0 ** (-5.0 - offset - np.arange(RET_HEADS, dtype=np.float64))))
    ld = np.repeat(ld, HEAD_DIM)[None, :]
    t = np.arange(CHUNK, dtype=np.float64)[:, None]
    s = (np.arange(256) % HEAD_DIM)[None, :].astype(np.float64)
    if reverse:
        b = (CHUNK - t) * ld
        dmat = np.where(s >= t, np.exp((s - t) * ld), 0.0)
    else:
        b = (t + 1.0) * ld
        dmat = np.where(s <= t, np.exp((t - s) * ld), 0.0)
    bend = CHUNK * ld
    f = lambda a: jnp.asarray(a, F32)
    return f(dmat), f(np.exp(b)), f(np.exp(bend - b)), f(np.exp(bend))


def _rope_tables(seq_lens, use_rope):
    n_pos = max(t for t, r in zip(seq_lens, use_rope) if r) if any(use_rope) else 0
    n_freq = HEAD_DIM // 4
    tok = np.arange(n_pos)
    rowp = (tok // GRID_W).astype(np.float32)
    colp = (tok % GRID_W).astype(np.float32)
    inv_freq = (ROPE_BASE ** (-np.arange(n_freq, dtype=np.float32) / n_freq)).astype(np.float32)
    ang = np.concatenate([rowp[:, None] * inv_freq, colp[:, None] * inv_freq], axis=-1)
    cos, sin = jnp.cos(jnp.asarray(ang)), jnp.sin(jnp.asarray(ang))
    cos_h = jnp.concatenate([cos, cos], axis=-1)
    sin_h = jnp.concatenate([-sin, sin], axis=-1)
    cos_t = jnp.concatenate([jnp.tile(cos_h, (1, RET_HEADS)), jnp.ones((TB, 256), F32)], axis=0)
    sin_t = jnp.concatenate([jnp.tile(sin_h, (1, RET_HEADS)), jnp.zeros((TB, 256), F32)], axis=0)
    return cos_t, sin_t, n_pos // TB


def mixer_direction(proj, edges, seq_lens, params, states, obwd, reverse):
    n_tok = proj.shape[0]
    n_seq = len(seq_lens)
    n_state = states[0].shape[0]
    tbl, n_blocks, _ = _block_table(seq_lens, reverse)
    tbl = np.concatenate([tbl[:, :6],
                          np.maximum(tbl[:, 0:1] - 1, 0),
                          np.minimum(tbl[:, 0:1] + 1, n_blocks - 1),
                          np.minimum(tbl[:, 1:2], n_state - 1),
                          (tbl[:, 1:2] < n_state).astype(np.int32)], axis=1).astype(np.int32)
    d = 1 if reverse else 0
    retd, reteq, retek, reteend = _retention_constants(reverse)
    rep = lambda a: jnp.repeat(a.astype(F32), HEAD_DIM)[None, :]
    expand = np.zeros((128, 512), np.float32)
    for h in range(SSD_HEADS):
        expand[8 * d + h, HEAD_DIM * h:HEAD_DIM * (h + 1)] = 1.0
    small = [
        _tri_matrix(reverse), jnp.asarray(expand, BF16),
        params["conv_w"].astype(F32),
        rep(params["d_skip"]),
        params["hg_norm_g"].astype(F32)[None, :], params["ssd_norm_g"].astype(F32)[None, :],
        params["ret_norm_g"].astype(F32)[None, :],
        retd, reteq, retek, reteend,
    ]
    full = lambda a: pl.BlockSpec(a.shape, lambda i, t: (0,) * a.ndim)
    in_specs = [
        pl.BlockSpec((TB, P_COLS), lambda i, t: (t[i, 0], 0)),
        pl.BlockSpec((8, 768), lambda i, t: (t[i, 6], 0)),
        pl.BlockSpec((8, 768), lambda i, t: (t[i, 7], 0)),
        pl.BlockSpec((8, 768), lambda i, t: (t[i, 0], 0)),
    ] + [full(a) for a in small] + [
        pl.BlockSpec((1, 64, 512), lambda i, t: (t[i, 8], 0, 0)),
        pl.BlockSpec((1, 64, 256), lambda i, t: (t[i, 8], 0, 0)),
        pl.BlockSpec((1, 64, 256), lambda i, t: (t[i, 8], 0, 0)),
    ]
    args = [proj, edges, edges, edges] + small + list(states)
    if not reverse:
        in_specs.append(pl.BlockSpec((TB, 1024), lambda i, t: (t[i, 0], 0)))
        args.append(obwd)
    out_dtype = F32 if reverse else BF16
    out_shape = [
        jax.ShapeDtypeStruct((n_tok, 1024), out_dtype),
        jax.ShapeDtypeStruct((n_seq, 64, 512), F32),
        jax.ShapeDtypeStruct((n_seq, 64, 256), F32),
        jax.ShapeDtypeStruct((n_seq, 64, 256), F32),
    ]
    out_specs = [
        pl.BlockSpec((TB, 1024), lambda i, t: (t[i, 0], 0)),
        pl.BlockSpec((1, 64, 512), lambda i, t: (t[i, 1], 0, 0)),
        pl.BlockSpec((1, 64, 256), lambda i, t: (t[i, 1], 0, 0)),
        pl.BlockSpec((1, 64, 256), lambda i, t: (t[i, 1], 0, 0)),
    ]
    scratch = [
        pltpu.VMEM((TB, 1280), F32),
        pltpu.VMEM((128, 512), F32),
        pltpu.VMEM((256, 256), F32),
        pltpu.VMEM((256, 256), F32),
    ]
    return pl.pallas_call(
        functools.partial(_mixer_kernel, reverse=reverse),
        grid_spec=pltpu.PrefetchScalarGridSpec(
            num_scalar_prefetch=1, grid=(n_blocks,), in_specs=in_specs, out_specs=out_specs,
            scratch_shapes=scratch),
        out_shape=out_shape,
        compiler_params=pltpu.CompilerParams(dimension_semantics=("arbitrary",), vmem_limit_bytes=48 * 1024 * 1024),
        name="mixer_bwd" if reverse else "mixer_fwd",
    )(jnp.asarray(tbl), *args)


ADA_TN = 1024


def _ada_kernel(c_ref, w_ref, b_ref, o_ref):
    s = _silu(c_ref[...]).astype(BF16)
    o_ref[0] = _dot(s, w_ref[0].astype(BF16)) + b_ref[0]


def ada_modulation(cond, w_ada, b_ada):
    n_c = cond.shape[0]
    n_l, d, n_out = w_ada.shape
    return pl.pallas_call(
        _ada_kernel,
        grid=(n_l, n_out // ADA_TN),
        in_specs=[pl.BlockSpec((n_c, d), lambda l, j: (0, 0)),
                  pl.BlockSpec((1, d, ADA_TN), lambda l, j: (l, 0, j)),
                  pl.BlockSpec((1, 1, ADA_TN), lambda l, j: (l, 0, j))],
        out_specs=pl.BlockSpec((1, n_c, ADA_TN), lambda l, j: (l, 0, j)),
        out_shape=jax.ShapeDtypeStruct((n_l, n_c, n_out), F32),
        name="ada_mod",
    )(cond, w_ada, b_ada.reshape(n_l, 1, n_out))


TM = 256


def _rms(x):
    return x * lax.rsqrt(jnp.mean(x * x, axis=-1, keepdims=True) + EPS)


def _load_tokens(ref, n_tok):
    return jnp.concatenate([ref[pl.ds(c, n_tok, stride=8), :] for c in range(8)], axis=1)


def _store_tokens(ref, val, n_tok):
    for c in range(8):
        ref[pl.ds(c, n_tok, stride=8), :] = val[:, 128 * c:128 * (c + 1)]


def _rope(a, cos, sin):
    half_lo = (lax.broadcasted_iota(jnp.int32, a.shape, 1) & 63) < 32
    ar = jnp.where(half_lo, pltpu.roll(a, 224, axis=1), pltpu.roll(a, 32, axis=1))
    return a * cos + ar * sin


def _inproj_kernel(crow_ref, ropeb_ref, x_ref, mod_ref, g_ref, w_ref, dtb_ref, alog_ref, lb_ref, cw_ref, cb_ref,
                   cos_ref, sin_ref, o_ref, e_ref):
    r = crow_ref[pl.program_id(0)]
    sh = mod_ref[pl.ds(r, 1), 0:1024]
    sc = mod_ref[pl.ds(r, 1), 1024:2048]
    h = _rms(_load_tokens(x_ref, TM)) * g_ref[...] * (1.0 + sc) + sh
    raw_all = _dot(h.astype(BF16), w_ref[...])

    def raw(col, width):
        return raw_all[:, col:col + width]

    cos = cos_ref[...]
    sin = sin_ref[...]
    o_ref[:, C_RET:C_RET + 256] = _rope(raw(R_RET, 256), cos, sin)
    o_ref[:, C_RET + 256:C_RET + 512] = _rope(raw(R_RET + 256, 256) * (HEAD_DIM ** -0.5), cos, sin)
    o_ref[:, C_RET + 512:C_RET + 768] = raw(R_RET + 512, 256)
    o_ref[:, C_RET + 768:C_RET + 1024] = _silu(raw(R_RET + 768, 256))
    o_ref[:, C_HG:C_HG + 256] = _silu(raw(R_HG, 256)) * (HEAD_DIM ** -0.5)
    o_ref[:, C_HG + 256:C_HG + 512] = raw(R_HG + 256, 256)
    for dirn in range(2):
        lb = lb_ref[dirn:dirn + 1, :]
        fr = raw(R_HG + 512 + 256 * dirn, 256)
        o_ref[:, C_HG + 512 + 256 * dirn:C_HG + 768 + 256 * dirn] = jnp.log(lb + (1.0 - lb) * _sigmoid(fr))
        o_ref[:, C_HG + 1024 + 256 * dirn:C_HG + 1280 + 256 * dirn] = (1.0 - lb) * _sigmoid(-fr)
    o_ref[:, C_HG + 1536:C_HG + 1792] = _silu(raw(R_HG + 1024, 256))
    o_ref[:, C_Z:C_Z + 512] = _silu(raw(R_Z, 512))
    dt = _softplus(raw(R_DT, 128) + dtb_ref[...])
    o_ref[:, C_DT:C_DT + 128] = dt
    o_ref[:, C_LA:C_LA + 128] = -dt * jnp.exp(alog_ref[...])

    xbc = raw(R_XBC, 768)
    row = lax.broadcasted_iota(jnp.int32, (CHUNK, 768), 0)
    zero_row = jnp.zeros((1, 768), F32)
    for s in range(TM // CHUNK):
        r0 = s * CHUNK
        xb = xbc[r0:r0 + CHUNK, :]
        before = xbc[r0 - 1:r0, :] if s > 0 else zero_row
        after = xbc[r0 + CHUNK:r0 + CHUNK + 1, :] if s < TM // CHUNK - 1 else zero_row
        xm1 = jnp.where(row == 0, before, pltpu.roll(xb, 1, axis=0))
        xp1 = jnp.where(row == CHUNK - 1, after, pltpu.roll(xb, CHUNK - 1, axis=0))
        pre = cw_ref[0:1, :] * xm1 + cw_ref[1:2, :] * xb + cw_ref[2:3, :] * xp1 + cb_ref[...]
        o_ref[r0:r0 + CHUNK, C_XBC:C_XBC + 768] = _silu(pre)
        if s == 0:
            e_ref[0:1, :] = xb[0:1, :]
            e_ref[2:3, :] = pre[0:1, :]
        if s == TM // CHUNK - 1:
            e_ref[1:2, :] = xb[CHUNK - 1:CHUNK, :]
            e_ref[3:4, :] = pre[CHUNK - 1:CHUNK, :]
    e_ref[4:8, :] = jnp.zeros((4, 768), F32)


def in_projection(x8, crow, rope_blk, mod, gain, w, dt_bias, a_log, lb, conv_w, conv_b, cos_t, sin_t):
    n_tok, d = x8.shape[0] // 8, D_MODEL
    compact = lambda a: jnp.zeros((1, 128), F32).at[0, :2 * SSD_HEADS].set(a.astype(F32).reshape(-1))
    const = lambda shape: pl.BlockSpec(shape, lambda i, c, rb: (0, 0))
    return pl.pallas_call(
        _inproj_kernel,
        grid_spec=pltpu.PrefetchScalarGridSpec(
            num_scalar_prefetch=2, grid=(n_tok // TM,),
            in_specs=[pl.BlockSpec((TM * 8, 128), lambda i, c, rb: (i, 0)),
                      const(mod.shape), const((1, d)), const((d, R_COLS)), const((1, 128)), const((1, 128)),
                      const((2, HG_W)), const((3, SSD_CONV_CH)), const((1, SSD_CONV_CH)),
                      pl.BlockSpec((TM, 256), lambda i, c, rb: (rb[i], 0)),
                      pl.BlockSpec((TM, 256), lambda i, c, rb: (rb[i], 0))],
            out_specs=[pl.BlockSpec((TM, P_COLS), lambda i, c, rb: (i, 0)),
                       pl.BlockSpec((8, SSD_CONV_CH), lambda i, c, rb: (i, 0))]),
        out_shape=[jax.ShapeDtypeStruct((n_tok, P_COLS), F32),
                   jax.ShapeDtypeStruct((n_tok // TM * 8, SSD_CONV_CH), F32)],
        compiler_params=pltpu.CompilerParams(dimension_semantics=("arbitrary",), vmem_limit_bytes=48 * 1024 * 1024),
        name="in_proj",
    )(crow, rope_blk, x8, mod, gain[None, :], w, compact(dt_bias), compact(a_log), lb.astype(F32),
      conv_w.astype(F32), conv_b.astype(F32)[None, :], cos_t, sin_t)


def _outproj_kernel(crow_ref, y_ref, x_ref, mod_ref, g_ref, wo_ref, wr_ref, x1_ref, aff_ref):
    r = crow_ref[pl.program_id(0)]
    g1 = mod_ref[pl.ds(r, 1), 2048:3072]
    sh = mod_ref[pl.ds(r, 1), 3072:4096]
    sc = mod_ref[pl.ds(r, 1), 4096:5120]
    x1 = _load_tokens(x_ref, TM) + g1 * _dot(y_ref[...], wo_ref[...])
    _store_tokens(x1_ref, x1, TM)
    h = _rms(x1) * g_ref[...] * (1.0 + sc) + sh
    lt = _dot_nt(wr_ref[...], h.astype(BF16))
    e = jnp.exp(lt - jnp.max(lt, axis=0, keepdims=True))
    aff_ref[...] = e / jnp.sum(e, axis=0, keepdims=True)


def out_projection(y, x8, crow, mod, gain, w_out, w_router):
    n_tok, d = x8.shape[0] // 8, D_MODEL
    wr = w_router.astype(BF16).T
    return pl.pallas_call(
        _outproj_kernel,
        grid_spec=pltpu.PrefetchScalarGridSpec(
            num_scalar_prefetch=1, grid=(n_tok // TM,),
            in_specs=[pl.BlockSpec((TM, d), lambda i, c: (i, 0)),
                      pl.BlockSpec((TM * 8, 128), lambda i, c: (i, 0)),
                      pl.BlockSpec(mod.shape, lambda i, c: (0, 0)),
                      pl.BlockSpec((1, d), lambda i, c: (0, 0)),
                      pl.BlockSpec((d, d), lambda i, c: (0, 0)),
                      pl.BlockSpec((N_EXPERTS, d), lambda i, c: (0, 0))],
            out_specs=[pl.BlockSpec((TM * 8, 128), lambda i, c: (i, 0)),
                       pl.BlockSpec((N_EXPERTS, TM), lambda i, c: (0, i))]),
        out_shape=[jax.ShapeDtypeStruct((n_tok * 8, 128), F32), jax.ShapeDtypeStruct((N_EXPERTS, n_tok), F32)],
        compiler_params=pltpu.CompilerParams(dimension_semantics=("arbitrary",)),
        name="out_proj",
    )(crow, y, x8, mod, gain[None, :], w_out, wr)


TK_BLK = 256
TK_JB = 64


def _topk_kernel(aff_ref, tri_ref, idx_ref, gate_ref, csum_scr, sel_scr, ti_scr, tg_scr, *, n, cap, cpad, npair):
    a = aff_ref[...]

    def count_ge(t):
        return jnp.sum((a >= t).astype(jnp.int32), axis=1, keepdims=True)

    def search(it, lo):
        cand = lo | (1 << (30 - it))
        return jnp.where(count_ge(pltpu.bitcast(cand, F32)) >= cap, cand, lo)

    lo_bits = lax.fori_loop(0, 31, search, jnp.zeros((npair, 1), jnp.int32))

    def refine(it, lh):
        lo, hi = lh
        mid = 0.5 * (lo + hi)
        ok = count_ge(mid) >= cap
        return jnp.where(ok, mid, lo), jnp.where(ok, hi, mid)

    lo_v, hi_v = lax.fori_loop(0, 32, refine, (pltpu.bitcast(lo_bits, F32), pltpu.bitcast(lo_bits + 1, F32)))
    gt = a >= hi_v
    eq = (a >= lo_v) & (a < hi_v)
    need = (cap - jnp.sum(gt.astype(jnp.int32), axis=1, keepdims=True)).astype(F32)

    tri = tri_ref[...]

    def prefix(mask):
        carry = jnp.zeros((npair, 1), F32)
        parts = []
        for kb in range(n // TK_BLK):
            blk = mask[:, kb * TK_BLK:(kb + 1) * TK_BLK].astype(BF16)
            loc = _dot(blk, tri) + carry
            parts.append(loc)
            carry = loc[:, TK_BLK - 1:TK_BLK]
        return jnp.concatenate(parts, axis=1)

    sel = gt | (eq & (prefix(eq) <= need))
    csum_scr[...] = prefix(sel)
    sel_scr[...] = jnp.where(sel, a, 0.0)
    lane = lax.broadcasted_iota(jnp.int32, (TK_JB, max(npair, 128)), 1)

    def per_expert(e, carry):
        cs = csum_scr[pl.ds(e, 1), :]
        av = sel_scr[pl.ds(e, 1), :]

        def per_block(jb, c2):
            j0 = pl.multiple_of(jb * TK_JB, TK_JB)
            j = (lax.broadcasted_iota(jnp.int32, (TK_JB, n), 0) + j0).astype(F32)
            idx_col = jnp.sum((cs <= j).astype(F32), axis=1, keepdims=True)
            gate_col = jnp.sum(jnp.where(cs == j + 1.0, av, 0.0), axis=1, keepdims=True)
            rows = pl.ds(j0, TK_JB)
            ti_scr[rows, :] = jnp.where(lane == e, idx_col, ti_scr[rows, :])
            tg_scr[rows, :] = jnp.where(lane == e, gate_col, tg_scr[rows, :])
            return c2

        return lax.fori_loop(0, cpad // TK_JB, per_block, carry)

    lax.fori_loop(0, npair, per_expert, 0)
    idx_ref[...] = ti_scr[...].T[0:npair, :].astype(jnp.int32)
    gate_ref[...] = tg_scr[...].T[0:npair, :]


def expert_choice_topk(aff_t, tok0, n_rows, n, rows_per_step):
    cap = EC_CAPACITY_FACTOR * n // N_EXPERTS
    cpad = max(cap, 128)
    npair = rows_per_step * N_EXPERTS
    plane = max(npair, 128)
    pairs = aff_t[:, tok0:tok0 + n_rows * n].reshape(N_EXPERTS, n_rows, n).transpose(1, 0, 2).reshape(n_rows * N_EXPERTS, n)
    t = np.arange(TK_BLK)
    tri = jnp.asarray(t[:, None] <= t[None, :], BF16)
    idx, gate = pl.pallas_call(
        functools.partial(_topk_kernel, n=n, cap=cap, cpad=cpad, npair=npair),
        grid=(n_rows // rows_per_step,),
        in_specs=[pl.BlockSpec((npair, n), lambda r: (r, 0)),
                  pl.BlockSpec((TK_BLK, TK_BLK), lambda r: (0, 0))],
        out_specs=[pl.BlockSpec((npair, cpad), lambda r: (r, 0)),
                   pl.BlockSpec((npair, cpad), lambda r: (r, 0))],
        out_shape=[jax.ShapeDtypeStruct((n_rows * N_EXPERTS, cpad), jnp.int32),
                   jax.ShapeDtypeStruct((n_rows * N_EXPERTS, cpad), F32)],
        scratch_shapes=[pltpu.VMEM((npair, n), F32), pltpu.VMEM((npair, n), F32),
                        pltpu.VMEM((cpad, plane), F32), pltpu.VMEM((cpad, plane), F32)],
        compiler_params=pltpu.CompilerParams(dimension_semantics=("arbitrary",)),
        name=f"topk_{n}",
    )(pairs, tri)
    return (idx[:, :cap].reshape(n_rows, N_EXPERTS, cap), gate[:, :cap].reshape(n_rows, N_EXPERTS, cap))


MOE_NT = 4096
MOE_C = 512
MOE_GROUP = 8


def _moe_kernel(crow_ref, idx_ref, gate_ref, x_ref, mod_ref, g_ref, wg_ref, wu_ref, wd_ref, o_ref, xs_scr, ys_scr):
    grp = pl.program_id(0)
    e = pl.program_id(1)
    r = crow_ref[grp]
    base = (grp * N_EXPERTS + e) * MOE_C
    x8 = x_ref.at[0]
    o8 = o_ref.at[0]

    def tile(i):
        return pl.ds(pl.multiple_of(i * 8, 8), 8)

    @pl.when(e == 0)
    def _():
        o_ref[...] = x_ref[...]

    def gather(j, c):
        xs_scr[tile(j), :] = x8[tile(idx_ref[base + j]), :]
        return c

    lax.fori_loop(0, MOE_C, gather, 0, unroll=8)

    sh = mod_ref[pl.ds(r, 1), 3072:4096]
    sc = mod_ref[pl.ds(r, 1), 4096:5120]
    g2 = mod_ref[pl.ds(r, 1), 5120:6144]
    h = (_rms(_load_tokens(xs_scr, MOE_C)) * g_ref[...] * (1.0 + sc) + sh).astype(BF16)
    hid = _silu(_dot(h, wg_ref[0].astype(BF16))) * _dot(h, wu_ref[0].astype(BF16))
    gate_col = jnp.broadcast_to(gate_ref[0], (128, MOE_C)).T[:, 0:1]
    _store_tokens(ys_scr, _dot(hid.astype(BF16), wd_ref[0].astype(BF16)) * g2 * gate_col, MOE_C)

    def scatter(grp_j, c):
        j0 = grp_j * MOE_GROUP
        toks = [idx_ref[base + j0 + k] for k in range(MOE_GROUP)]
        vals = [o8[tile(toks[k]), :] + ys_scr[tile(j0 + k), :] for k in range(MOE_GROUP)]
        for k in range(MOE_GROUP):
            o8[tile(toks[k]), :] = vals[k]
        return c

    lax.fori_loop(0, MOE_C // MOE_GROUP, scatter, 0)


def moe_ffn(x8, idx, gate, crow_grp, mod, gain, w_gate, w_up, w_down):
    n_tok, d = x8.shape[0] // 8, D_MODEL
    n_grp = n_tok // MOE_NT
    xg = x8.reshape(n_grp, MOE_NT * 8, 128)
    out = pl.pallas_call(
        _moe_kernel,
        grid_spec=pltpu.PrefetchScalarGridSpec(
            num_scalar_prefetch=2, grid=(n_grp, N_EXPERTS),
            in_specs=[pl.BlockSpec((1, 1, MOE_C), lambda g, e, c, ix: (g * N_EXPERTS + e, 0, 0)),
                      pl.BlockSpec((1, MOE_NT * 8, 128), lambda g, e, c, ix: (g, 0, 0), pipeline_mode=pl.Buffered(1)),
                      pl.BlockSpec(mod.shape, lambda g, e, c, ix: (0, 0)),
                      pl.BlockSpec((1, d), lambda g, e, c, ix: (0, 0)),
                      pl.BlockSpec((1, d, D_EXPERT), lambda g, e, c, ix: (e, 0, 0)),
                      pl.BlockSpec((1, d, D_EXPERT), lambda g, e, c, ix: (e, 0, 0)),
                      pl.BlockSpec((1, D_EXPERT, d), lambda g, e, c, ix: (e, 0, 0))],
            out_specs=pl.BlockSpec((1, MOE_NT * 8, 128), lambda g, e, c, ix: (g, 0, 0), pipeline_mode=pl.Buffered(1)),
            scratch_shapes=[pltpu.VMEM((MOE_C * 8, 128), F32), pltpu.VMEM((MOE_C * 8, 128), F32)]),
        out_shape=jax.ShapeDtypeStruct((n_grp, MOE_NT * 8, 128), F32),
        compiler_params=pltpu.CompilerParams(dimension_semantics=("arbitrary", "arbitrary"),
                                             vmem_limit_bytes=56 * 1024 * 1024),
        name="moe_ffn",
    )(crow_grp, idx, gate, xg, mod, gain[None, :], w_gate, w_up, w_down)
    return out.reshape(n_tok * 8, 128)


def _final_kernel(x_ref, g_ref, o_ref):
    o_ref[...] = _rms(_load_tokens(x_ref, TM)) * g_ref[...]


def final_norm(x8, gain, tok0, n_out):
    blk0 = tok0 // TM
    return pl.pallas_call(
        _final_kernel,
        grid=(n_out // TM,),
        in_specs=[pl.BlockSpec((TM * 8, 128), lambda i: (blk0 + i, 0)),
                  pl.BlockSpec((1, D_MODEL), lambda i: (0, 0))],
        out_specs=pl.BlockSpec((TM, D_MODEL), lambda i: (i, 0)),
        out_shape=jax.ShapeDtypeStruct((n_out, D_MODEL), F32),
        name="final_norm",
    )(x8, gain[None, :])


def _ssd_to_k(s):
    return s.transpose(0, 2, 1, 3).reshape(s.shape[0], 64, 512)


def _ssd_from_k(m):
    return m.reshape(m.shape[0], 64, 8, 64).transpose(0, 2, 1, 3)


def _ret_to_k(s):
    return s.transpose(0, 2, 1, 3).reshape(s.shape[0], 64, 256)


def _ret_from_k(m):
    return m.reshape(m.shape[0], 64, 4, 64).transpose(0, 2, 1, 3)


def _hg_to_k(s):
    return s.transpose(0, 3, 1, 2).reshape(s.shape[0], 64, 256)


def _hg_from_k(m):
    return m.reshape(m.shape[0], 64, 4, 64).transpose(0, 2, 3, 1)


def _arrange_w_in(w):
    hg = w[:, 0:1280]
    z = w[:, 1280:1792]
    xbc = w[:, 1792:2560]
    dt = jnp.pad(w[:, 2560:2576], ((0, 0), (0, 128 - 2 * SSD_HEADS)))
    ret = w[:, 2576:3600]
    return jnp.concatenate([xbc, z, dt, hg, ret], axis=1).astype(BF16)


def kernel(x_prompt, x_sample, state_hgrn, state_ssd, state_ret, c, c_ctx, norm1_g, norm2_g, final_g,
           w_ada, b_ada, w_in, w_out, hg_lb, hg_norm_g, ssd_conv_w, ssd_conv_b, ssd_dt_bias, ssd_a_log,
           ssd_d, ssd_norm_g, ret_norm_g, w_router, w_gate, w_up, w_down):
    n_ctx, t_ctx, d = x_prompt.shape
    n_smp, t_smp, _ = x_sample.shape
    n_tok_s = n_smp * t_smp
    n_tok = n_tok_s + n_ctx * t_ctx
    seq_lens = [t_smp] * n_smp + [t_ctx] * n_ctx
    use_rope = [True] * n_smp + [False] * n_ctx

    x = jnp.concatenate([x_sample.reshape(n_tok_s * 8, 128), x_prompt.reshape(n_ctx * t_ctx * 8, 128)], axis=0)
    cond = jnp.zeros((16, d), F32).at[0].set(c_ctx).at[1:1 + n_smp].set(c)
    crow_tm = jnp.asarray([i * TM // t_smp + 1 if i * TM < n_tok_s else 0 for i in range(n_tok // TM)], jnp.int32)
    crow_grp = jnp.asarray([g + 1 if g < n_smp else 0 for g in range(n_tok // MOE_NT)], jnp.int32)
    cos_t, sin_t, ident_blk = _rope_tables(seq_lens, use_rope)
    rope_blk = jnp.asarray([i % (t_smp // TM) if i * TM < n_tok_s else ident_blk for i in range(n_tok // TM)], jnp.int32)

    lb_p = jax.nn.softmax(hg_lb.astype(F32), axis=0)
    lb_all = jnp.cumsum(lb_p, axis=0) - lb_p[0]
    mod_all = ada_modulation(cond, w_ada, b_ada)

    rows_per_grp = MOE_NT // t_ctx
    n_grp_ctx = n_ctx // rows_per_grp
    new_states = {"hg": [], "ssd": [], "ret": []}
    for l in range(DEPTH):
        mod = mod_all[l]
        proj, edges = in_projection(x, crow_tm, rope_blk, mod, norm1_g[l], _arrange_w_in(w_in[l]), ssd_dt_bias[l],
                                    ssd_a_log[l], lb_all[l], ssd_conv_w[l], ssd_conv_b[l], cos_t, sin_t)
        params = dict(conv_w=ssd_conv_w[l], d_skip=ssd_d[l], hg_norm_g=hg_norm_g[l],
                      ssd_norm_g=ssd_norm_g[l], ret_norm_g=ret_norm_g[l])
        outs = []
        for dirn in (1, 0):
            st = (_ssd_to_k(state_ssd[:, l, dirn]), _ret_to_k(state_ret[:, l, dirn]), _hg_to_k(state_hgrn[:, l, dirn]))
            outs.append(mixer_direction(proj, edges, seq_lens, params, st, outs[0][0] if outs else None, dirn == 1))
        (_, b_ssd, b_ret, b_hg), (y, f_ssd, f_ret, f_hg) = outs
        new_states["ssd"].append(jnp.stack([_ssd_from_k(f_ssd[n_smp:]), _ssd_from_k(b_ssd[n_smp:])], axis=1))
        new_states["ret"].append(jnp.stack([_ret_from_k(f_ret[n_smp:]), _ret_from_k(b_ret[n_smp:])], axis=1))
        new_states["hg"].append(jnp.stack([_hg_from_k(f_hg[n_smp:]), _hg_from_k(b_hg[n_smp:])], axis=1))

        x1, aff_t = out_projection(y, x, crow_tm, mod, norm2_g[l], w_out[l].astype(BF16), w_router[l])
        idx_s, gate_s = expert_choice_topk(aff_t, 0, n_smp, t_smp, 1)
        idx_c, gate_c = expert_choice_topk(aff_t, n_tok_s, n_ctx, t_ctx, rows_per_grp)
        cap_c = idx_c.shape[-1]
        idx_c = idx_c + (jnp.arange(n_ctx, dtype=jnp.int32) % rows_per_grp)[:, None, None] * t_ctx
        regroup = lambda a: a.reshape(n_grp_ctx, rows_per_grp, N_EXPERTS, cap_c).transpose(0, 2, 1, 3).reshape(
            n_grp_ctx, N_EXPERTS, rows_per_grp * cap_c)
        idx = jnp.concatenate([idx_s, regroup(idx_c)], axis=0).reshape(-1)
        gate = jnp.concatenate([gate_s, regroup(gate_c)], axis=0).reshape(-1, 1, MOE_C)
        x = moe_ffn(x1, idx, gate, crow_grp, mod, norm2_g[l], w_gate[l], w_up[l], w_down[l])

    y_sample = final_norm(x, final_g, 0, n_tok_s).reshape(n_smp, t_smp, d)
    y_prompt = final_norm(x, final_g, n_tok_s, n_ctx * t_ctx).reshape(n_ctx, t_ctx, d)
    return (y_prompt, y_sample, jnp.stack(new_states["hg"], axis=1), jnp.stack(new_states["ssd"], axis=1),
            jnp.stack(new_states["ret"], axis=1))
```

```python
import functools
import math

import jax
import jax.numpy as jnp
import numpy as np
from jax import lax
from jax.experimental import pallas as pl
from jax.experimental.pallas import tpu as pltpu

D_MODEL = 1024
DEPTH = 2
GRID_W = 64
CHUNK = 64
EPS = 1e-6
HEAD_DIM = 64
HG_W = 256
SSD_W = 512
RET_W = 256
HG_HEADS = 4
SSD_HEADS = 8
SSD_N = 64
SSD_GROUPS = 2
SSD_CONV_CH = 768
RET_HEADS = 4
RET_BWD_OFFSET = 0.5
ROPE_BASE = 10000.0
N_EXPERTS = 16
EC_CAPACITY_FACTOR = 2
D_EXPERT = 512

R_XBC = 0
R_Z = 768
R_DT = 1280
R_HG = 1408
R_RET = 2688
R_COLS = 3712

C_XBC = 0
C_Z = 768
C_DT = 1280
C_LA = 1408
C_HG = 1536
C_RET = 3328
P_COLS = 4352

TB = 256
NCH = TB // CHUNK
SUB = 8
NSUB = CHUNK // SUB
EXP_CLAMP = 80.0

F32 = jnp.float32
BF16 = jnp.bfloat16


def _sigmoid(x):
    return 1.0 / (1.0 + jnp.exp(-x))


def _silu(x):
    return x * (0.5 + 0.5 * jnp.tanh(0.5 * x))


def _softplus(x):
    return jnp.maximum(x, 0.0) + jnp.log1p(jnp.exp(-jnp.abs(x)))


def _dot(a, b):
    return jnp.dot(a, b, preferred_element_type=F32)


def _dot_nt(a, b):
    return lax.dot_general(a, b, (((1,), (1,)), ((), ())), preferred_element_type=F32)


def _dot_tn(a, b):
    return lax.dot_general(a, b, (((0,), (0,)), ((), ())), preferred_element_type=F32)


def _split3(x):
    hi = x.astype(BF16)
    r = x - hi.astype(F32)
    mid = r.astype(BF16)
    lo = (r - mid.astype(F32)).astype(BF16)
    return hi, mid, lo


def _seg_mean(x, ones_bd):
    hi = x.astype(BF16)
    lo = (x - hi.astype(F32)).astype(BF16)
    return (_dot(hi, ones_bd) + _dot(lo, ones_bd)) * (1.0 / HEAD_DIM)


def _tile_rows(x, n):
    return jnp.concatenate([x] * n, axis=0)


def _mixer_kernel(tbl_ref, proj_ref, prev_ref, next_ref, own_ref, tri_ref, expand_ref,
                  convw_ref, dskip_ref,
                  hgn_ref, ssdn_ref, retn_ref, retd_ref, reteq_ref, retek_ref, reteend_ref,
                  sssd_in_ref, sret_in_ref, shg_in_ref, *rest, reverse):
    if reverse:
        (y_ref, sssd_out_ref, sret_out_ref, shg_out_ref, b_scr, s_ssd, s_ret, s_hg) = rest
        obwd_ref = None
    else:
        (obwd_ref, y_ref, sssd_out_ref, sret_out_ref, shg_out_ref, b_scr, s_ssd, s_ret, s_hg) = rest

    step = pl.program_id(0)
    is_first = tbl_ref[step, 2]
    keep_prev = (1 - tbl_ref[step, 3]).astype(F32)
    keep_next = (1 - tbl_ref[step, 4]).astype(F32)

    r256 = lax.broadcasted_iota(jnp.int32, (256, 256), 0)
    c256 = lax.broadcasted_iota(jnp.int32, (256, 256), 1)
    bd = (r256 >> 6) == (c256 >> 6)
    gmask_s = (lax.broadcasted_iota(jnp.int32, (128, 512), 0) >> 6) == (lax.broadcasted_iota(jnp.int32, (128, 512), 1) >> 8)

    @pl.when(is_first == 1)
    def _():
        has = tbl_ref[step, 9].astype(F32)
        s_ssd[...] = jnp.where(gmask_s, _tile_rows(sssd_in_ref[0], 2), 0.0) * has
        s_ret[...] = jnp.where(bd, _tile_rows(sret_in_ref[0], 4), 0.0) * has
        s_hg[...] = jnp.where(bd, _tile_rows(shg_in_ref[0], 4), 0.0) * has

    lf_col = C_HG + (768 if reverse else 512)
    key_col = C_HG + (1280 if reverse else 1024)

    xbc_first = _silu(own_ref[2:3, :] + convw_ref[0:1, :] * prev_ref[1:2, :] * keep_prev)
    xbc_last = _silu(own_ref[3:4, :] + convw_ref[2:3, :] * next_ref[0:1, :] * keep_next)

    def split2(a):
        hi = a.astype(BF16)
        return hi, (a - hi.astype(F32)).astype(BF16)

    la_hi, la_lo = split2(proj_ref[:, C_LA:C_LA + 128])
    lf_hi, lf_lo = split2(proj_ref[:, lf_col:lf_col + 256])
    cum = _dot(tri_ref[...], jnp.concatenate([la_hi, la_lo, lf_hi, lf_lo], axis=1))
    b_scr[:, 512:768] = cum[:, 256:512] + cum[:, 512:768]
    c_hi, c_lo = split2(cum[:, 0:128] + cum[:, 128:256])
    d_hi, d_lo = split2(proj_ref[:, C_DT:C_DT + 128])
    spread = _dot(jnp.concatenate([c_hi, c_lo, d_hi, d_lo], axis=0), expand_ref[...])
    b_scr[:, 0:512] = spread[0:TB] + spread[TB:2 * TB]
    b_scr[:, 768:1280] = spread[2 * TB:3 * TB] + spread[3 * TB:4 * TB]

    gmask_b = (lax.broadcasted_iota(jnp.int32, (512, 128), 0) >> 8) == (lax.broadcasted_iota(jnp.int32, (512, 128), 1) >> 6)
    t512 = lax.broadcasted_iota(jnp.int32, (CHUNK, 512), 0)
    s512 = lax.broadcasted_iota(jnp.int32, (CHUNK, 512), 1) & 63
    t256 = lax.broadcasted_iota(jnp.int32, (CHUNK, 256), 0)
    s256 = lax.broadcasted_iota(jnp.int32, (CHUNK, 256), 1) & 63
    if reverse:
        causal512, causal256 = s512 >= t512, s256 >= t256
    else:
        causal512, causal256 = s512 <= t512, s256 <= t256
    diag512 = s512 == t512
    sl8 = lax.broadcasted_iota(jnp.int32, (SUB, 256), 1) & 63
    row768 = lax.broadcasted_iota(jnp.int32, (CHUNK, 768), 0)
    ones_bd = bd.astype(BF16)
    end_row = 0 if reverse else CHUNK - 1

    for k in range(NCH):
        c = (NCH - 1 - k) if reverse else k
        r0 = c * CHUNK
        rows = pl.ds(r0, CHUNK)

        xbc = proj_ref[rows, C_XBC:C_XBC + 768]
        if c == 0:
            xbc = jnp.where(row768 == 0, xbc_first, xbc)
        if c == NCH - 1:
            xbc = jnp.where(row768 == CHUNK - 1, xbc_last, xbc)
        bb = b_scr[rows, 0:512]
        dtc = b_scr[rows, 768:1280]
        xs = xbc[:, 0:512]
        bm = xbc[:, 512:640].astype(BF16)
        cm = xbc[:, 640:768].astype(BF16)
        xdt = xs * dtc
        bend = b_scr[pl.ds(r0 + end_row, 1), 0:512]
        rvec = jnp.sum(jnp.where(diag512, bb, 0.0), axis=0, keepdims=True)
        dmat = jnp.where(causal512, jnp.exp(jnp.minimum(bb - rvec, 0.0)), 0.0)
        rhs_b = jnp.where(gmask_b, _tile_rows(bm, 8), jnp.zeros((), BF16))
        a_ssd = (_dot_nt(cm, rhs_b) * dmat).astype(BF16)
        xdt_b = xdt.astype(BF16)
        o_parts = []
        for g in range(SSD_GROUPS):
            vbd = jnp.where(bd, _tile_rows(xdt_b[:, g * 256:(g + 1) * 256], 4), jnp.zeros((), BF16))
            o_parts.append(_dot(a_ssd[:, g * 256:(g + 1) * 256], vbd))
        o_ssd = jnp.concatenate(o_parts, axis=1) + jnp.exp(bb) * _dot(cm, s_ssd[...].astype(BF16))
        u = _dot_tn(bm, (xdt * jnp.exp(bend - bb)).astype(BF16))
        s_ssd[...] = s_ssd[...] * jnp.exp(bend) + jnp.where(gmask_s, u, 0.0)
        if reverse:
            y_ref[rows, 256:768] = o_ssd
        else:
            y = o_ssd + obwd_ref[rows, 256:768] + xs * dskip_ref[...]
            y = y * proj_ref[rows, C_Z:C_Z + 512]
            y = y * lax.rsqrt(jnp.mean(y * y, axis=-1, keepdims=True) + EPS) * ssdn_ref[...]
            y_ref[rows, 256:768] = y.astype(y_ref.dtype)

        q = proj_ref[rows, C_RET:C_RET + 256]
        kk = proj_ref[rows, C_RET + 256:C_RET + 512]
        v = proj_ref[rows, C_RET + 512:C_RET + 768].astype(BF16)
        kbd = jnp.where(bd, _tile_rows(kk.astype(BF16), 4), jnp.zeros((), BF16))
        a_ret = (_dot_nt(q.astype(BF16), kbd) * retd_ref[...]).astype(BF16)
        vbd = jnp.where(bd, _tile_rows(v, 4), jnp.zeros((), BF16))
        o_ret = _dot(a_ret, vbd) + _dot((q * reteq_ref[...]).astype(BF16), s_ret[...].astype(BF16))
        u = _dot_tn((kk * retek_ref[...]).astype(BF16), v)
        s_ret[...] = s_ret[...] * reteend_ref[...] + jnp.where(bd, u, 0.0)
        if reverse:
            y_ref[rows, 768:1024] = o_ret
        else:
            o_ret = o_ret + obwd_ref[rows, 768:1024]
            xc = o_ret - _seg_mean(o_ret, ones_bd)
            var = _seg_mean(xc * xc, ones_bd)
            gate = proj_ref[rows, C_RET + 768:C_RET + 1024]
            y_ref[rows, 768:1024] = (xc * lax.rsqrt(var + EPS) * retn_ref[...] * gate).astype(y_ref.dtype)

        bb = b_scr[rows, 512:768]
        q = proj_ref[rows, C_HG:C_HG + 256]
        v = proj_ref[rows, C_HG + 256:C_HG + 512].astype(BF16)
        key = proj_ref[rows, key_col:key_col + 256]
        bend = b_scr[pl.ds(r0 + end_row, 1), 512:768]

        def brow(r):
            return b_scr[pl.ds(r0 + r, 1), 512:768]

        zero_row = jnp.zeros((1, 256), F32)
        if reverse:
            rs = [brow(SUB * i + SUB) if i < NSUB - 1 else zero_row for i in range(NSUB)]
            re = [brow(SUB * j) for j in range(NSUB)]
        else:
            rs = [brow(SUB * i - 1) if i > 0 else zero_row for i in range(NSUB)]
            re = [brow(SUB * j + SUB - 1) for j in range(NSUB)]
        qh = [q[SUB * i:SUB * (i + 1)] * jnp.exp(bb[SUB * i:SUB * (i + 1)] - rs[i]) for i in range(NSUB)]
        kh = jnp.concatenate(
            [key[SUB * j:SUB * (j + 1)] * jnp.exp(re[j] - bb[SUB * j:SUB * (j + 1)]) for j in range(NSUB)], axis=0)
        pairs = [(i, j) for i in range(NSUB) for j in range(NSUB) if (j >= i if reverse else j <= i)]
        lhs = jnp.concatenate(
            [qh[i] * jnp.exp(jnp.minimum(rs[i] - re[j], EXP_CLAMP)) for (i, j) in pairs], axis=0).astype(BF16)
        kbd = jnp.where(bd, _tile_rows(kh.astype(BF16), 4), jnp.zeros((), BF16))
        prod = _dot_nt(lhs, kbd)
        a_tiles = []
        for i in range(NSUB):
            a_i = jnp.zeros((SUB, 256), F32)
            for p, (pi, pj) in enumerate(pairs):
                if pi == i:
                    a_i = jnp.where((sl8 >> 3) == pj, prod[SUB * p:SUB * (p + 1)], a_i)
            a_tiles.append(a_i)
        a_hg = jnp.where(causal256, jnp.concatenate(a_tiles, axis=0), 0.0).astype(BF16)
        vbd = jnp.where(bd, _tile_rows(v, 4), jnp.zeros((), BF16))
        o_hg = _dot(a_hg, vbd) + _dot_nt((q * jnp.exp(bb)).astype(BF16), s_hg[...].astype(BF16))
        u = _dot_tn(v, (key * jnp.exp(bend - bb)).astype(BF16))
        s_hg[...] = s_hg[...] * jnp.exp(bend) + jnp.where(bd, u, 0.0)
        if reverse:
            y_ref[rows, 0:256] = o_hg
        else:
            o_hg = o_hg + obwd_ref[rows, 0:256]
            ms = _seg_mean(o_hg * o_hg, ones_bd)
            gate = proj_ref[rows, C_HG + 1536:C_HG + 1792]
            y_ref[rows, 0:256] = (o_hg * lax.rsqrt(ms + EPS) * hgn_ref[...] * gate).astype(y_ref.dtype)

    sssd_out_ref[0] = s_ssd[0:64, :] + s_ssd[64:128, :]
    sret_out_ref[0] = s_ret[0:64, :] + s_ret[64:128, :] + s_ret[128:192, :] + s_ret[192:256, :]
    shg_out_ref[0] = s_hg[0:64, :] + s_hg[64:128, :] + s_hg[128:192, :] + s_hg[192:256, :]


def _block_table(seq_lens, reverse):
    rows = []
    blk0 = 0
    max_blocks = max(t // TB for t in seq_lens)
    for s, t in enumerate(seq_lens):
        nb = t // TB
        for k in range(nb):
            pos = (nb - 1 - k) if reverse else k
            rows.append([blk0 + pos, s, int(k == 0), int(pos == 0), int(pos == nb - 1), pos])
        blk0 += nb
    return np.asarray(rows, np.int32), blk0, max_blocks


def _tri_matrix(reverse):
    t = np.arange(TB)
    m = (t[None, :] >= t[:, None]) if reverse else (t[None, :] <= t[:, None])
    m = m & ((t[None, :] // CHUNK) == (t[:, None] // CHUNK))
    return jnp.asarray(m, BF16)


def _retention_constants(reverse):
    offset = RET_BWD_OFFSET if reverse else 0.0
    ld = np.log1p(-(2.0 ** (-5.0 - offset - np.arange(RET_HEADS, dtype=np.float64))))
    ld = np.repeat(ld, HEAD_DIM)[None, :]
    t = np.arange(CHUNK, dtype=np.float64)[:, None]
    s = (np.arange(256) % HEAD_DIM)[None, :].astype(np.float64)
    if reverse:
        b = (CHUNK - t) * ld
        dmat = np.where(s >= t, np.exp((s - t) * ld), 0.0)
    else:
        b = (t + 1.0) * ld
        dmat = np.where(s <= t, np.exp((t - s) * ld), 0.0)
    bend = CHUNK * ld
    f = lambda a: jnp.asarray(a, F32)
    return f(dmat), f(np.exp(b)), f(np.exp(bend - b)), f(np.exp(bend))


def _rope_tables(seq_lens, use_rope):
    n_pos = max(t for t, r in zip(seq_lens, use_rope) if r) if any(use_rope) else 0
    n_freq = HEAD_DIM // 4
    tok = np.arange(n_pos)
    rowp = (tok // GRID_W).astype(np.float32)
    colp = (tok % GRID_W).astype(np.float32)
    inv_freq = (ROPE_BASE ** (-np.arange(n_freq, dtype=np.float32) / n_freq)).astype(np.float32)
    ang = np.concatenate([rowp[:, None] * inv_freq, colp[:, None] * inv_freq], axis=-1)
    cos, sin = jnp.cos(jnp.asarray(ang)), jnp.sin(jnp.asarray(ang))
    cos_h = jnp.concatenate([cos, cos], axis=-1)
    sin_h = jnp.concatenate([-sin, sin], axis=-1)
    cos_t = jnp.concatenate([jnp.tile(cos_h, (1, RET_HEADS)), jnp.ones((TB, 256), F32)], axis=0)
    sin_t = jnp.concatenate([jnp.tile(sin_h, (1, RET_HEADS)), jnp.zeros((TB, 256), F32)], axis=0)
    return cos_t, sin_t, n_pos // TB


def mixer_direction(proj, edges, seq_lens, params, states, obwd, reverse):
    n_tok = proj.shape[0]
    n_seq = len(seq_lens)
    n_state = states[0].shape[0]
    tbl, n_blocks, _ = _block_table(seq_lens, reverse)
    tbl = np.concatenate([tbl[:, :6],
                          np.maximum(tbl[:, 0:1] - 1, 0),
                          np.minimum(tbl[:, 0:1] + 1, n_blocks - 1),
                          np.minimum(tbl[:, 1:2], n_state - 1),
                          (tbl[:, 1:2] < n_state).astype(np.int32)], axis=1).astype(np.int32)
    d = 1 if reverse else 0
    retd, reteq, retek, reteend = _retention_constants(reverse)
    rep = lambda a: jnp.repeat(a.astype(F32), HEAD_DIM)[None, :]
    expand = np.zeros((128, 512), np.float32)
    for h in range(SSD_HEADS):
        expand[8 * d + h, HEAD_DIM * h:HEAD_DIM * (h + 1)] = 1.0
    small = [
        _tri_matrix(reverse), jnp.asarray(expand, BF16),
        params["conv_w"].astype(F32),
        rep(params["d_skip"]),
        params["hg_norm_g"].astype(F32)[None, :], params["ssd_norm_g"].astype(F32)[None, :],
        params["ret_norm_g"].astype(F32)[None, :],
        retd, reteq, retek, reteend,
    ]
    full = lambda a: pl.BlockSpec(a.shape, lambda i, t: (0,) * a.ndim)
    in_specs = [
        pl.BlockSpec((TB, P_COLS), lambda i, t: (t[i, 0], 0)),
        pl.BlockSpec((8, 768), lambda i, t: (t[i, 6], 0)),
        pl.BlockSpec((8, 768), lambda i, t: (t[i, 7], 0)),
        pl.BlockSpec((8, 768), lambda i, t: (t[i, 0], 0)),
    ] + [full(a) for a in small] + [
        pl.BlockSpec((1, 64, 512), lambda i, t: (t[i, 8], 0, 0)),
        pl.BlockSpec((1, 64, 256), lambda i, t: (t[i, 8], 0, 0)),
        pl.BlockSpec((1, 64, 256), lambda i, t: (t[i, 8], 0, 0)),
    ]
    args = [proj, edges, edges, edges] + small + list(states)
    if not reverse:
        in_specs.append(pl.BlockSpec((TB, 1024), lambda i, t: (t[i, 0], 0)))
        args.append(obwd)
    out_dtype = F32 if reverse else BF16
    out_shape = [
        jax.ShapeDtypeStruct((n_tok, 1024), out_dtype),
        jax.ShapeDtypeStruct((n_seq, 64, 512), F32),
        jax.ShapeDtypeStruct((n_seq, 64, 256), F32),
        jax.ShapeDtypeStruct((n_seq, 64, 256), F32),
    ]
    out_specs = [
        pl.BlockSpec((TB, 1024), lambda i, t: (t[i, 0], 0)),
        pl.BlockSpec((1, 64, 512), lambda i, t: (t[i, 1], 0, 0)),
        pl.BlockSpec((1, 64, 256), lambda i, t: (t[i, 1], 0, 0)),
        pl.BlockSpec((1, 64, 256), lambda i, t: (t[i, 1], 0, 0)),
    ]
    scratch = [
        pltpu.VMEM((TB, 1280), F32),
        pltpu.VMEM((128, 512), F32),
        pltpu.VMEM((256, 256), F32),
        pltpu.VMEM((256, 256), F32),
    ]
    return pl.pallas_call(
        functools.partial(_mixer_kernel, reverse=reverse),
        grid_spec=pltpu.PrefetchScalarGridSpec(
            num_scalar_prefetch=1, grid=(n_blocks,), in_specs=in_specs, out_specs=out_specs,
            scratch_shapes=scratch),
        out_shape=out_shape,
        compiler_params=pltpu.CompilerParams(dimension_semantics=("arbitrary",), vmem_limit_bytes=48 * 1024 * 1024),
        name="mixer_bwd" if reverse else "mixer_fwd",
    )(jnp.asarray(tbl), *args)


ADA_TN = 1024


def _ada_kernel(c_ref, w_ref, b_ref, o_ref):
    s = _silu(c_ref[...]).astype(BF16)
    o_ref[0] = _dot(s, w_ref[0].astype(BF16)) + b_ref[0]


def ada_modulation(cond, w_ada, b_ada):
    n_c = cond.shape[0]
    n_l, d, n_out = w_ada.shape
    return pl.pallas_call(
        _ada_kernel,
        grid=(n_l, n_out // ADA_TN),
        in_specs=[pl.BlockSpec((n_c, d), lambda l, j: (0, 0)),
                  pl.BlockSpec((1, d, ADA_TN), lambda l, j: (l, 0, j)),
                  pl.BlockSpec((1, 1, ADA_TN), lambda l, j: (l, 0, j))],
        out_specs=pl.BlockSpec((1, n_c, ADA_TN), lambda l, j: (l, 0, j)),
        out_shape=jax.ShapeDtypeStruct((n_l, n_c, n_out), F32),
        name="ada_mod",
    )(cond, w_ada, b_ada.reshape(n_l, 1, n_out))


TM = 256


def _rms(x):
    return x * lax.rsqrt(jnp.mean(x * x, axis=-1, keepdims=True) + EPS)


def _load_tokens(ref, n_tok):
    return jnp.concatenate([ref[pl.ds(c, n_tok, stride=8), :] for c in range(8)], axis=1)


def _store_tokens(ref, val, n_tok):
    for c in range(8):
        ref[pl.ds(c, n_tok, stride=8), :] = val[:, 128 * c:128 * (c + 1)]


def _rope(a, cos, sin):
    half_lo = (lax.broadcasted_iota(jnp.int32, a.shape, 1) & 63) < 32
    ar = jnp.where(half_lo, pltpu.roll(a, 224, axis=1), pltpu.roll(a, 32, axis=1))
    return a * cos + ar * sin


def _inproj_kernel(crow_ref, ropeb_ref, x_ref, mod_ref, g_ref, w_ref, dtb_ref, alog_ref, lb_ref, cw_ref, cb_ref,
                   cos_ref, sin_ref, o_ref, e_ref, *, tiles):
    r = crow_ref[pl.program_id(0)]
    sh = mod_ref[pl.ds(r, 1), 0:1024]
    sc = mod_ref[pl.ds(r, 1), 1024:2048]
    x = _load_tokens(x_ref, TM) if tiles else x_ref[...]
    h = _rms(x) * g_ref[...] * (1.0 + sc) + sh
    raw_all = _dot(h.astype(BF16), w_ref[...])

    def raw(col, width):
        return raw_all[:, col:col + width]

    cos = cos_ref[...]
    sin = sin_ref[...]
    o_ref[:, C_RET:C_RET + 256] = _rope(raw(R_RET, 256), cos, sin)
    o_ref[:, C_RET + 256:C_RET + 512] = _rope(raw(R_RET + 256, 256) * (HEAD_DIM ** -0.5), cos, sin)
    o_ref[:, C_RET + 512:C_RET + 768] = raw(R_RET + 512, 256)
    o_ref[:, C_RET + 768:C_RET + 1024] = _silu(raw(R_RET + 768, 256))
    o_ref[:, C_HG:C_HG + 256] = _silu(raw(R_HG, 256)) * (HEAD_DIM ** -0.5)
    o_ref[:, C_HG + 256:C_HG + 512] = raw(R_HG + 256, 256)
    for dirn in range(2):
        lb = lb_ref[dirn:dirn + 1, :]
        fr = raw(R_HG + 512 + 256 * dirn, 256)
        o_ref[:, C_HG + 512 + 256 * dirn:C_HG + 768 + 256 * dirn] = jnp.log(lb + (1.0 - lb) * _sigmoid(fr))
        o_ref[:, C_HG + 1024 + 256 * dirn:C_HG + 1280 + 256 * dirn] = (1.0 - lb) * _sigmoid(-fr)
    o_ref[:, C_HG + 1536:C_HG + 1792] = _silu(raw(R_HG + 1024, 256))
    o_ref[:, C_Z:C_Z + 512] = _silu(raw(R_Z, 512))
    dt = _softplus(raw(R_DT, 128) + dtb_ref[...])
    o_ref[:, C_DT:C_DT + 128] = dt
    o_ref[:, C_LA:C_LA + 128] = -dt * jnp.exp(alog_ref[...])

    xbc = raw(R_XBC, 768)
    row = lax.broadcasted_iota(jnp.int32, (CHUNK, 768), 0)
    zero_row = jnp.zeros((1, 768), F32)
    for s in range(TM // CHUNK):
        r0 = s * CHUNK
        xb = xbc[r0:r0 + CHUNK, :]
        before = xbc[r0 - 1:r0, :] if s > 0 else zero_row
        after = xbc[r0 + CHUNK:r0 + CHUNK + 1, :] if s < TM // CHUNK - 1 else zero_row
        xm1 = jnp.where(row == 0, before, pltpu.roll(xb, 1, axis=0))
        xp1 = jnp.where(row == CHUNK - 1, after, pltpu.roll(xb, CHUNK - 1, axis=0))
        pre = cw_ref[0:1, :] * xm1 + cw_ref[1:2, :] * xb + cw_ref[2:3, :] * xp1 + cb_ref[...]
        o_ref[r0:r0 + CHUNK, C_XBC:C_XBC + 768] = _silu(pre)
        if s == 0:
            e_ref[0:1, :] = xb[0:1, :]
            e_ref[2:3, :] = pre[0:1, :]
        if s == TM // CHUNK - 1:
            e_ref[1:2, :] = xb[CHUNK - 1:CHUNK, :]
            e_ref[3:4, :] = pre[CHUNK - 1:CHUNK, :]
    e_ref[4:8, :] = jnp.zeros((4, 768), F32)


def in_projection(x8, crow, rope_blk, mod, gain, w, dt_bias, a_log, lb, conv_w, conv_b, cos_t, sin_t):
    tiles = x8.shape[1] == 128
    n_tok, d = (x8.shape[0] // 8 if tiles else x8.shape[0]), D_MODEL
    compact = lambda a: jnp.zeros((1, 128), F32).at[0, :2 * SSD_HEADS].set(a.astype(F32).reshape(-1))
    const = lambda shape: pl.BlockSpec(shape, lambda i, c, rb: (0, 0))
    return pl.pallas_call(
        functools.partial(_inproj_kernel, tiles=tiles),
        grid_spec=pltpu.PrefetchScalarGridSpec(
            num_scalar_prefetch=2, grid=(n_tok // TM,),
            in_specs=[pl.BlockSpec((TM * 8, 128) if tiles else (TM, d), lambda i, c, rb: (i, 0)),
                      const(mod.shape), const((1, d)), const((d, R_COLS)), const((1, 128)), const((1, 128)),
                      const((2, HG_W)), const((3, SSD_CONV_CH)), const((1, SSD_CONV_CH)),
                      pl.BlockSpec((TM, 256), lambda i, c, rb: (rb[i], 0)),
                      pl.BlockSpec((TM, 256), lambda i, c, rb: (rb[i], 0))],
            out_specs=[pl.BlockSpec((TM, P_COLS), lambda i, c, rb: (i, 0)),
                       pl.BlockSpec((8, SSD_CONV_CH), lambda i, c, rb: (i, 0))]),
        out_shape=[jax.ShapeDtypeStruct((n_tok, P_COLS), F32),
                   jax.ShapeDtypeStruct((n_tok // TM * 8, SSD_CONV_CH), F32)],
        compiler_params=pltpu.CompilerParams(dimension_semantics=("arbitrary",), vmem_limit_bytes=48 * 1024 * 1024),
        name="in_proj",
    )(crow, rope_blk, x8, mod, gain[None, :], w, compact(dt_bias), compact(a_log), lb.astype(F32),
      conv_w.astype(F32), conv_b.astype(F32)[None, :], cos_t, sin_t)


def _outproj_kernel(crow_ref, y_ref, x_ref, mod_ref, g_ref, wo_ref, wr_ref, x1_ref, aff_ref, *, tiles):
    r = crow_ref[pl.program_id(0)]
    g1 = mod_ref[pl.ds(r, 1), 2048:3072]
    sh = mod_ref[pl.ds(r, 1), 3072:4096]
    sc = mod_ref[pl.ds(r, 1), 4096:5120]
    x = _load_tokens(x_ref, TM) if tiles else x_ref[...]
    x1 = x + g1 * _dot(y_ref[...], wo_ref[...])
    _store_tokens(x1_ref, x1, TM)
    h = _rms(x1) * g_ref[...] * (1.0 + sc) + sh
    lt = _dot_nt(wr_ref[...], h.astype(BF16))
    e = jnp.exp(lt - jnp.max(lt, axis=0, keepdims=True))
    aff_ref[...] = e / jnp.sum(e, axis=0, keepdims=True)


def out_projection(y, x8, crow, mod, gain, w_out, w_router):
    tiles = x8.shape[1] == 128
    n_tok, d = (x8.shape[0] // 8 if tiles else x8.shape[0]), D_MODEL
    wr = w_router.astype(BF16).T
    return pl.pallas_call(
        functools.partial(_outproj_kernel, tiles=tiles),
        grid_spec=pltpu.PrefetchScalarGridSpec(
            num_scalar_prefetch=1, grid=(n_tok // TM,),
            in_specs=[pl.BlockSpec((TM, d), lambda i, c: (i, 0)),
                      pl.BlockSpec((TM * 8, 128) if tiles else (TM, d), lambda i, c: (i, 0)),
                      pl.BlockSpec(mod.shape, lambda i, c: (0, 0)),
                      pl.BlockSpec((1, d), lambda i, c: (0, 0)),
                      pl.BlockSpec((d, d), lambda i, c: (0, 0)),
                      pl.BlockSpec((N_EXPERTS, d), lambda i, c: (0, 0))],
            out_specs=[pl.BlockSpec((TM * 8, 128), lambda i, c: (i, 0)),
                       pl.BlockSpec((N_EXPERTS, TM), lambda i, c: (0, i))]),
        out_shape=[jax.ShapeDtypeStruct((n_tok * 8, 128), F32), jax.ShapeDtypeStruct((N_EXPERTS, n_tok), F32)],
        compiler_params=pltpu.CompilerParams(dimension_semantics=("arbitrary",)),
        name="out_proj",
    )(crow, y, x8, mod, gain[None, :], w_out, wr)


TK_BLK = 256
TK_JB = 64


def _topk_kernel(aff_ref, tri_ref, idx_ref, gate_ref, csum_scr, sel_scr, ti_scr, tg_scr, *, n, cap, cpad, npair):
    a = aff_ref[...]

    def count_ge(t):
        return jnp.sum((a >= t).astype(jnp.int32), axis=1, keepdims=True)

    def search(it, lo):
        cand = lo | (1 << (30 - it))
        return jnp.where(count_ge(pltpu.bitcast(cand, F32)) >= cap, cand, lo)

    lo_bits = lax.fori_loop(0, 31, search, jnp.zeros((npair, 1), jnp.int32))

    def refine(it, lh):
        lo, hi = lh
        mid = 0.5 * (lo + hi)
        ok = count_ge(mid) >= cap
        return jnp.where(ok, mid, lo), jnp.where(ok, hi, mid)

    lo_v, hi_v = lax.fori_loop(0, 32, refine, (pltpu.bitcast(lo_bits, F32), pltpu.bitcast(lo_bits + 1, F32)))
    gt = a >= hi_v
    eq = (a >= lo_v) & (a < hi_v)
    need = (cap - jnp.sum(gt.astype(jnp.int32), axis=1, keepdims=True)).astype(F32)

    tri = tri_ref[...]

    def prefix(mask):
        carry = jnp.zeros((npair, 1), F32)
        parts = []
        for kb in range(n // TK_BLK):
            blk = mask[:, kb * TK_BLK:(kb + 1) * TK_BLK].astype(BF16)
            loc = _dot(blk, tri) + carry
            parts.append(loc)
            carry = loc[:, TK_BLK - 1:TK_BLK]
        return jnp.concatenate(parts, axis=1)

    sel = gt | (eq & (prefix(eq) <= need))
    csum_scr[...] = prefix(sel)
    sel_scr[...] = jnp.where(sel, a, 0.0)
    lane = lax.broadcasted_iota(jnp.int32, (TK_JB, max(npair, 128)), 1)

    def per_expert(e, carry):
        cs = csum_scr[pl.ds(e, 1), :]
        av = sel_scr[pl.ds(e, 1), :]

        def per_block(jb, c2):
            j0 = pl.multiple_of(jb * TK_JB, TK_JB)
            j = (lax.broadcasted_iota(jnp.int32, (TK_JB, n), 0) + j0).astype(F32)
            idx_col = jnp.sum((cs <= j).astype(F32), axis=1, keepdims=True)
            gate_col = jnp.sum(jnp.where(cs == j + 1.0, av, 0.0), axis=1, keepdims=True)
            rows = pl.ds(j0, TK_JB)
            ti_scr[rows, :] = jnp.where(lane == e, idx_col, ti_scr[rows, :])
            tg_scr[rows, :] = jnp.where(lane == e, gate_col, tg_scr[rows, :])
            return c2

        return lax.fori_loop(0, cpad // TK_JB, per_block, carry)

    lax.fori_loop(0, npair, per_expert, 0)
    idx_ref[...] = ti_scr[...].T[0:npair, :].astype(jnp.int32)
    gate_ref[...] = tg_scr[...].T[0:npair, :]


def expert_choice_topk(aff_t, tok0, n_rows, n, rows_per_step):
    cap = EC_CAPACITY_FACTOR * n // N_EXPERTS
    cpad = max(cap, 128)
    npair = rows_per_step * N_EXPERTS
    plane = max(npair, 128)
    pairs = aff_t[:, tok0:tok0 + n_rows * n].reshape(N_EXPERTS, n_rows, n).transpose(1, 0, 2).reshape(n_rows * N_EXPERTS, n)
    t = np.arange(TK_BLK)
    tri = jnp.asarray(t[:, None] <= t[None, :], BF16)
    idx, gate = pl.pallas_call(
        functools.partial(_topk_kernel, n=n, cap=cap, cpad=cpad, npair=npair),
        grid=(n_rows // rows_per_step,),
        in_specs=[pl.BlockSpec((npair, n), lambda r: (r, 0)),
                  pl.BlockSpec((TK_BLK, TK_BLK), lambda r: (0, 0))],
        out_specs=[pl.BlockSpec((npair, cpad), lambda r: (r, 0)),
                   pl.BlockSpec((npair, cpad), lambda r: (r, 0))],
        out_shape=[jax.ShapeDtypeStruct((n_rows * N_EXPERTS, cpad), jnp.int32),
                   jax.ShapeDtypeStruct((n_rows * N_EXPERTS, cpad), F32)],
        scratch_shapes=[pltpu.VMEM((npair, n), F32), pltpu.VMEM((npair, n), F32),
                        pltpu.VMEM((cpad, plane), F32), pltpu.VMEM((cpad, plane), F32)],
        compiler_params=pltpu.CompilerParams(dimension_semantics=("arbitrary",)),
        name=f"topk_{n}",
    )(pairs, tri)
    return (idx[:, :cap].reshape(n_rows, N_EXPERTS, cap), gate[:, :cap].reshape(n_rows, N_EXPERTS, cap))


MOE_NT = 4096
MOE_C = 512
MOE_GROUP = 8


def _moe_kernel(crow_ref, idx_ref, gate_ref, x_ref, mod_ref, g_ref, wg_ref, wu_ref, wd_ref, o_ref, xs_scr, ys_scr):
    grp = pl.program_id(0)
    e = pl.program_id(1)
    r = crow_ref[grp]
    base = (grp * N_EXPERTS + e) * MOE_C
    x8 = x_ref.at[0]
    o8 = o_ref.at[0]

    def tile(i):
        return pl.ds(pl.multiple_of(i * 8, 8), 8)

    def tok_tile(p):
        return pl.ds(pl.multiple_of(idx_ref[p], 8), 8)

    @pl.when(e == 0)
    def _():
        o_ref[...] = x_ref[...]

    def gather(j, c):
        xs_scr[tile(j), :] = x8[tok_tile(base + j), :]
        return c

    lax.fori_loop(0, MOE_C, gather, 0, unroll=8)

    sh = mod_ref[pl.ds(r, 1), 3072:4096]
    sc = mod_ref[pl.ds(r, 1), 4096:5120]
    g2 = mod_ref[pl.ds(r, 1), 5120:6144]
    h = (_rms(_load_tokens(xs_scr, MOE_C)) * g_ref[...] * (1.0 + sc) + sh).astype(BF16)
    hid = _silu(_dot(h, wg_ref[0].astype(BF16))) * _dot(h, wu_ref[0].astype(BF16))
    gate_col = jnp.broadcast_to(gate_ref[0], (128, MOE_C)).T[:, 0:1]
    _store_tokens(ys_scr, _dot(hid.astype(BF16), wd_ref[0].astype(BF16)) * g2 * gate_col, MOE_C)

    def scatter(grp_j, c):
        j0 = grp_j * MOE_GROUP
        toks = [tok_tile(base + j0 + k) for k in range(MOE_GROUP)]
        vals = [o8[toks[k], :] + ys_scr[tile(j0 + k), :] for k in range(MOE_GROUP)]
        for k in range(MOE_GROUP):
            o8[toks[k], :] = vals[k]
        return c

    lax.fori_loop(0, MOE_C // MOE_GROUP, scatter, 0)


def moe_ffn(x8, idx, gate, crow_grp, mod, gain, w_gate, w_up, w_down, layer):
    n_tok, d = x8.shape[0] // 8, D_MODEL
    n_grp = n_tok // MOE_NT
    xg = x8.reshape(n_grp, MOE_NT * 8, 128)
    out = pl.pallas_call(
        _moe_kernel,
        grid_spec=pltpu.PrefetchScalarGridSpec(
            num_scalar_prefetch=2, grid=(n_grp, N_EXPERTS),
            in_specs=[pl.BlockSpec((1, 1, MOE_C), lambda g, e, c, ix: (g * N_EXPERTS + e, 0, 0)),
                      pl.BlockSpec((1, MOE_NT * 8, 128), lambda g, e, c, ix: (g, 0, 0), pipeline_mode=pl.Buffered(1)),
                      pl.BlockSpec(mod.shape, lambda g, e, c, ix: (0, 0)),
                      pl.BlockSpec((1, d), lambda g, e, c, ix: (0, 0)),
                      pl.BlockSpec((None, 1, d, D_EXPERT), lambda g, e, c, ix: (layer, e, 0, 0)),
                      pl.BlockSpec((None, 1, d, D_EXPERT), lambda g, e, c, ix: (layer, e, 0, 0)),
                      pl.BlockSpec((None, 1, D_EXPERT, d), lambda g, e, c, ix: (layer, e, 0, 0))],
            out_specs=pl.BlockSpec((1, MOE_NT * 8, 128), lambda g, e, c, ix: (g, 0, 0), pipeline_mode=pl.Buffered(1)),
            scratch_shapes=[pltpu.VMEM((MOE_C * 8, 128), F32), pltpu.VMEM((MOE_C * 8, 128), F32)]),
        out_shape=jax.ShapeDtypeStruct((n_grp, MOE_NT * 8, 128), F32),
        compiler_params=pltpu.CompilerParams(dimension_semantics=("arbitrary", "arbitrary"),
                                             vmem_limit_bytes=56 * 1024 * 1024),
        name="moe_ffn",
    )(crow_grp, idx, gate, xg, mod, gain[None, :], w_gate, w_up, w_down)
    return out.reshape(n_tok * 8, 128)


def _final_kernel(x_ref, g_ref, o_ref):
    o_ref[...] = _rms(_load_tokens(x_ref, TM)) * g_ref[...]


def final_norm(x8, gain, tok0, n_out):
    blk0 = tok0 // TM
    return pl.pallas_call(
        _final_kernel,
        grid=(n_out // TM,),
        in_specs=[pl.BlockSpec((TM * 8, 128), lambda i: (blk0 + i, 0)),
                  pl.BlockSpec((1, D_MODEL), lambda i: (0, 0))],
        out_specs=pl.BlockSpec((TM, D_MODEL), lambda i: (i, 0)),
        out_shape=jax.ShapeDtypeStruct((n_out, D_MODEL), F32),
        name="final_norm",
    )(x8, gain[None, :])


def _ssd_to_k(s):
    return s.transpose(0, 2, 1, 3).reshape(s.shape[0], 64, 512)


def _ssd_from_k(m):
    return m.reshape(m.shape[0], 64, 8, 64).transpose(0, 2, 1, 3)


def _ret_to_k(s):
    return s.transpose(0, 2, 1, 3).reshape(s.shape[0], 64, 256)


def _ret_from_k(m):
    return m.reshape(m.shape[0], 64, 4, 64).transpose(0, 2, 1, 3)


def _hg_to_k(s):
    return s.transpose(0, 3, 1, 2).reshape(s.shape[0], 64, 256)


def _hg_from_k(m):
    return m.reshape(m.shape[0], 64, 4, 64).transpose(0, 2, 3, 1)


def _arrange_w_in(w):
    hg = w[:, 0:1280]
    z = w[:, 1280:1792]
    xbc = w[:, 1792:2560]
    dt = jnp.pad(w[:, 2560:2576], ((0, 0), (0, 128 - 2 * SSD_HEADS)))
    ret = w[:, 2576:3600]
    return jnp.concatenate([xbc, z, dt, hg, ret], axis=1).astype(BF16)


def kernel(x_prompt, x_sample, state_hgrn, state_ssd, state_ret, c, c_ctx, norm1_g, norm2_g, final_g,
           w_ada, b_ada, w_in, w_out, hg_lb, hg_norm_g, ssd_conv_w, ssd_conv_b, ssd_dt_bias, ssd_a_log,
           ssd_d, ssd_norm_g, ret_norm_g, w_router, w_gate, w_up, w_down):
    n_ctx, t_ctx, d = x_prompt.shape
    n_smp, t_smp, _ = x_sample.shape
    n_tok_s = n_smp * t_smp
    n_tok = n_tok_s + n_ctx * t_ctx
    seq_lens = [t_smp] * n_smp + [t_ctx] * n_ctx
    use_rope = [True] * n_smp + [False] * n_ctx

    x = jnp.concatenate([x_sample.reshape(n_tok_s, d), x_prompt.reshape(n_ctx * t_ctx, d)], axis=0)
    cond = jnp.zeros((16, d), F32).at[0].set(c_ctx).at[1:1 + n_smp].set(c)
    crow_tm = jnp.asarray([i * TM // t_smp + 1 if i * TM < n_tok_s else 0 for i in range(n_tok // TM)], jnp.int32)
    crow_grp = jnp.asarray([g + 1 if g < n_smp else 0 for g in range(n_tok // MOE_NT)], jnp.int32)
    cos_t, sin_t, ident_blk = _rope_tables(seq_lens, use_rope)
    rope_blk = jnp.asarray([i % (t_smp // TM) if i * TM < n_tok_s else ident_blk for i in range(n_tok // TM)], jnp.int32)

    lb_p = jax.nn.softmax(hg_lb.astype(F32), axis=0)
    lb_all = jnp.cumsum(lb_p, axis=0) - lb_p[0]
    mod_all = ada_modulation(cond, w_ada, b_ada)

    rows_per_grp = MOE_NT // t_ctx
    n_grp_ctx = n_ctx // rows_per_grp
    new_states = {"hg": [], "ssd": [], "ret": []}
    for l in range(DEPTH):
        mod = mod_all[l]
        proj, edges = in_projection(x, crow_tm, rope_blk, mod, norm1_g[l], _arrange_w_in(w_in[l]), ssd_dt_bias[l],
                                    ssd_a_log[l], lb_all[l], ssd_conv_w[l], ssd_conv_b[l], cos_t, sin_t)
        params = dict(conv_w=ssd_conv_w[l], d_skip=ssd_d[l], hg_norm_g=hg_norm_g[l],
                      ssd_norm_g=ssd_norm_g[l], ret_norm_g=ret_norm_g[l])
        outs = []
        for dirn in (1, 0):
            st = (_ssd_to_k(state_ssd[:, l, dirn]), _ret_to_k(state_ret[:, l, dirn]), _hg_to_k(state_hgrn[:, l, dirn]))
            outs.append(mixer_direction(proj, edges, seq_lens, params, st, outs[0][0] if outs else None, dirn == 1))
        (_, b_ssd, b_ret, b_hg), (y, f_ssd, f_ret, f_hg) = outs
        new_states["ssd"].append(jnp.stack([_ssd_from_k(f_ssd[n_smp:]), _ssd_from_k(b_ssd[n_smp:])], axis=1))
        new_states["ret"].append(jnp.stack([_ret_from_k(f_ret[n_smp:]), _ret_from_k(b_ret[n_smp:])], axis=1))
        new_states["hg"].append(jnp.stack([_hg_from_k(f_hg[n_smp:]), _hg_from_k(b_hg[n_smp:])], axis=1))

        x1, aff_t = out_projection(y, x, crow_tm, mod, norm2_g[l], w_out[l].astype(BF16), w_router[l])
        idx_s, gate_s = expert_choice_topk(aff_t, 0, n_smp, t_smp, 1)
        idx_c, gate_c = expert_choice_topk(aff_t, n_tok_s, n_ctx, t_ctx, rows_per_grp)
        cap_c = idx_c.shape[-1]
        idx_c = idx_c + (jnp.arange(n_ctx, dtype=jnp.int32) % rows_per_grp)[:, None, None] * t_ctx
        regroup = lambda a: a.reshape(n_grp_ctx, rows_per_grp, N_EXPERTS, cap_c).transpose(0, 2, 1, 3).reshape(
            n_grp_ctx, N_EXPERTS, rows_per_grp * cap_c)
        idx = jnp.concatenate([idx_s, regroup(idx_c)], axis=0).reshape(-1) * 8
        gate = jnp.concatenate([gate_s, regroup(gate_c)], axis=0).reshape(-1, 1, MOE_C)
        x = moe_ffn(x1, idx, gate, crow_grp, mod, norm2_g[l], w_gate, w_up, w_down, l)

    y_sample = final_norm(x, final_g, 0, n_tok_s).reshape(n_smp, t_smp, d)
    y_prompt = final_norm(x, final_g, n_tok_s, n_ctx * t_ctx).reshape(n_ctx, t_ctx, d)
    return (y_prompt, y_sample, jnp.stack(new_states["hg"], axis=1), jnp.stack(new_states["ssd"], axis=1),
            jnp.stack(new_states["ret"], axis=1))
```

```python
import functools
import math

import jax
import jax.numpy as jnp
import numpy as np
from jax import lax
from jax.experimental import pallas as pl
from jax.experimental.pallas import tpu as pltpu

D_MODEL = 1024
DEPTH = 2
GRID_W = 64
CHUNK = 64
EPS = 1e-6
HEAD_DIM = 64
HG_W = 256
SSD_W = 512
RET_W = 256
HG_HEADS = 4
SSD_HEADS = 8
SSD_N = 64
SSD_GROUPS = 2
SSD_CONV_CH = 768
RET_HEADS = 4
RET_BWD_OFFSET = 0.5
ROPE_BASE = 10000.0
N_EXPERTS = 16
EC_CAPACITY_FACTOR = 2
D_EXPERT = 512

R_XBC = 0
R_Z = 768
R_DT = 1280
R_HG = 1408
R_RET = 2688
R_COLS = 3712

C_XBC = 0
C_Z = 768
C_DT = 1280
C_LA = 1408
C_HG = 1536
C_RET = 3328
P_COLS = 4352

TB = 256
NCH = TB // CHUNK
SUB = 8
NSUB = CHUNK // SUB
EXP_CLAMP = 80.0

F32 = jnp.float32
BF16 = jnp.bfloat16


def _sigmoid(x):
    return 1.0 / (1.0 + jnp.exp(-x))


def _silu(x):
    return x * (0.5 + 0.5 * jnp.tanh(0.5 * x))


def _softplus(x):
    return jnp.maximum(x, 0.0) + jnp.log1p(jnp.exp(-jnp.abs(x)))


def _dot(a, b):
    return jnp.dot(a, b, preferred_element_type=F32)


def _dot_nt(a, b):
    return lax.dot_general(a, b, (((1,), (1,)), ((), ())), preferred_element_type=F32)


def _dot_tn(a, b):
    return lax.dot_general(a, b, (((0,), (0,)), ((), ())), preferred_element_type=F32)


def _split3(x):
    hi = x.astype(BF16)
    r = x - hi.astype(F32)
    mid = r.astype(BF16)
    lo = (r - mid.astype(F32)).astype(BF16)
    return hi, mid, lo


def _seg_mean(x, ones_bd):
    hi = x.astype(BF16)
    lo = (x - hi.astype(F32)).astype(BF16)
    return (_dot(hi, ones_bd) + _dot(lo, ones_bd)) * (1.0 / HEAD_DIM)


def _tile_rows(x, n):
    return jnp.concatenate([x] * n, axis=0)


def _mixer_kernel(tbl_ref, proj_ref, prev_ref, next_ref, own_ref, tri_ref, expand_ref,
                  convw_ref, dskip_ref,
                  hgn_ref, ssdn_ref, retn_ref, retd_ref, reteq_ref, retek_ref, reteend_ref,
                  sssd_in_ref, sret_in_ref, shg_in_ref, *rest, reverse):
    if reverse:
        (y_ref, sssd_out_ref, sret_out_ref, shg_out_ref, b_scr, s_ssd, s_ret, s_hg) = rest
        obwd_ref = None
    else:
        (obwd_ref, y_ref, sssd_out_ref, sret_out_ref, shg_out_ref, b_scr, s_ssd, s_ret, s_hg) = rest

    step = pl.program_id(0)
    is_first = tbl_ref[step, 2]
    keep_prev = (1 - tbl_ref[step, 3]).astype(F32)
    keep_next = (1 - tbl_ref[step, 4]).astype(F32)

    r256 = lax.broadcasted_iota(jnp.int32, (256, 256), 0)
    c256 = lax.broadcasted_iota(jnp.int32, (256, 256), 1)
    bd = (r256 >> 6) == (c256 >> 6)
    gmask_s = (lax.broadcasted_iota(jnp.int32, (128, 512), 0) >> 6) == (lax.broadcasted_iota(jnp.int32, (128, 512), 1) >> 8)

    @pl.when(is_first == 1)
    def _():
        has = tbl_ref[step, 9].astype(F32)
        s_ssd[...] = jnp.where(gmask_s, _tile_rows(sssd_in_ref[0], 2), 0.0) * has
        s_ret[...] = jnp.where(bd, _tile_rows(sret_in_ref[0], 4), 0.0) * has
        s_hg[...] = jnp.where(bd, _tile_rows(shg_in_ref[0], 4), 0.0) * has

    lf_col = C_HG + (768 if reverse else 512)
    key_col = C_HG + (1280 if reverse else 1024)

    xbc_first = _silu(own_ref[2:3, :] + convw_ref[0:1, :] * prev_ref[1:2, :] * keep_prev)
    xbc_last = _silu(own_ref[3:4, :] + convw_ref[2:3, :] * next_ref[0:1, :] * keep_next)

    def split2(a):
        hi = a.astype(BF16)
        return hi, (a - hi.astype(F32)).astype(BF16)

    la_hi, la_lo = split2(proj_ref[:, C_LA:C_LA + 128])
    lf_hi, lf_lo = split2(proj_ref[:, lf_col:lf_col + 256])
    cum = _dot(tri_ref[...], jnp.concatenate([la_hi, la_lo, lf_hi, lf_lo], axis=1))
    b_scr[:, 512:768] = cum[:, 256:512] + cum[:, 512:768]
    c_hi, c_lo = split2(cum[:, 0:128] + cum[:, 128:256])
    d_hi, d_lo = split2(proj_ref[:, C_DT:C_DT + 128])
    spread = _dot(jnp.concatenate([c_hi, c_lo, d_hi, d_lo], axis=0), expand_ref[...])
    b_scr[:, 0:512] = spread[0:TB] + spread[TB:2 * TB]
    b_scr[:, 768:1280] = spread[2 * TB:3 * TB] + spread[3 * TB:4 * TB]

    gmask_b = (lax.broadcasted_iota(jnp.int32, (512, 128), 0) >> 8) == (lax.broadcasted_iota(jnp.int32, (512, 128), 1) >> 6)
    t512 = lax.broadcasted_iota(jnp.int32, (CHUNK, 512), 0)
    s512 = lax.broadcasted_iota(jnp.int32, (CHUNK, 512), 1) & 63
    t256 = lax.broadcasted_iota(jnp.int32, (CHUNK, 256), 0)
    s256 = lax.broadcasted_iota(jnp.int32, (CHUNK, 256), 1) & 63
    if reverse:
        causal512, causal256 = s512 >= t512, s256 >= t256
    else:
        causal512, causal256 = s512 <= t512, s256 <= t256
    diag512 = s512 == t512
    sl8 = lax.broadcasted_iota(jnp.int32, (SUB, 256), 1) & 63
    row768 = lax.broadcasted_iota(jnp.int32, (CHUNK, 768), 0)
    ones_bd = bd.astype(BF16)
    end_row = 0 if reverse else CHUNK - 1

    for k in range(NCH):
        c = (NCH - 1 - k) if reverse else k
        r0 = c * CHUNK
        rows = pl.ds(r0, CHUNK)

        xbc = proj_ref[rows, C_XBC:C_XBC + 768]
        if c == 0:
            xbc = jnp.where(row768 == 0, xbc_first, xbc)
        if c == NCH - 1:
            xbc = jnp.where(row768 == CHUNK - 1, xbc_last, xbc)
        bb = b_scr[rows, 0:512]
        dtc = b_scr[rows, 768:1280]
        xs = xbc[:, 0:512]
        bm = xbc[:, 512:640].astype(BF16)
        cm = xbc[:, 640:768].astype(BF16)
        xdt = xs * dtc
        bend = b_scr[pl.ds(r0 + end_row, 1), 0:512]
        rvec = jnp.sum(jnp.where(diag512, bb, 0.0), axis=0, keepdims=True)
        dmat = jnp.where(causal512, jnp.exp(jnp.minimum(bb - rvec, 0.0)), 0.0)
        rhs_b = jnp.where(gmask_b, _tile_rows(bm, 8), jnp.zeros((), BF16))
        a_ssd = (_dot_nt(cm, rhs_b) * dmat).astype(BF16)
        xdt_b = xdt.astype(BF16)
        o_parts = []
        for g in range(SSD_GROUPS):
            vbd = jnp.where(bd, _tile_rows(xdt_b[:, g * 256:(g + 1) * 256], 4), jnp.zeros((), BF16))
            o_parts.append(_dot(a_ssd[:, g * 256:(g + 1) * 256], vbd))
        o_ssd = jnp.concatenate(o_parts, axis=1) + jnp.exp(bb) * _dot(cm, s_ssd[...].astype(BF16))
        u = _dot_tn(bm, (xdt * jnp.exp(bend - bb)).astype(BF16))
        s_ssd[...] = s_ssd[...] * jnp.exp(bend) + jnp.where(gmask_s, u, 0.0)
        if reverse:
            y_ref[rows, 256:768] = o_ssd
        else:
            y = o_ssd + obwd_ref[rows, 256:768] + xs * dskip_ref[...]
            y = y * proj_ref[rows, C_Z:C_Z + 512]
            y = y * lax.rsqrt(jnp.mean(y * y, axis=-1, keepdims=True) + EPS) * ssdn_ref[...]
            y_ref[rows, 256:768] = y.astype(y_ref.dtype)

        q = proj_ref[rows, C_RET:C_RET + 256]
        kk = proj_ref[rows, C_RET + 256:C_RET + 512]
        v = proj_ref[rows, C_RET + 512:C_RET + 768].astype(BF16)
        kbd = jnp.where(bd, _tile_rows(kk.astype(BF16), 4), jnp.zeros((), BF16))
        a_ret = (_dot_nt(q.astype(BF16), kbd) * retd_ref[...]).astype(BF16)
        vbd = jnp.where(bd, _tile_rows(v, 4), jnp.zeros((), BF16))
        o_ret = _dot(a_ret, vbd) + _dot((q * reteq_ref[...]).astype(BF16), s_ret[...].astype(BF16))
        u = _dot_tn((kk * retek_ref[...]).astype(BF16), v)
        s_ret[...] = s_ret[...] * reteend_ref[...] + jnp.where(bd, u, 0.0)
        if reverse:
            y_ref[rows, 768:1024] = o_ret
        else:
            o_ret = o_ret + obwd_ref[rows, 768:1024]
            xc = o_ret - _seg_mean(o_ret, ones_bd)
            var = _seg_mean(xc * xc, ones_bd)
            gate = proj_ref[rows, C_RET + 768:C_RET + 1024]
            y_ref[rows, 768:1024] = (xc * lax.rsqrt(var + EPS) * retn_ref[...] * gate).astype(y_ref.dtype)

        bb = b_scr[rows, 512:768]
        q = proj_ref[rows, C_HG:C_HG + 256]
        v = proj_ref[rows, C_HG + 256:C_HG + 512].astype(BF16)
        key = proj_ref[rows, key_col:key_col + 256]
        bend = b_scr[pl.ds(r0 + end_row, 1), 512:768]

        def brow(r):
            return b_scr[pl.ds(r0 + r, 1), 512:768]

        zero_row = jnp.zeros((1, 256), F32)
        if reverse:
            rs = [brow(SUB * i + SUB) if i < NSUB - 1 else zero_row for i in range(NSUB)]
            re = [brow(SUB * j) for j in range(NSUB)]
        else:
            rs = [brow(SUB * i - 1) if i > 0 else zero_row for i in range(NSUB)]
            re = [brow(SUB * j + SUB - 1) for j in range(NSUB)]
        qh = [q[SUB * i:SUB * (i + 1)] * jnp.exp(bb[SUB * i:SUB * (i + 1)] - rs[i]) for i in range(NSUB)]
        kh = jnp.concatenate(
            [key[SUB * j:SUB * (j + 1)] * jnp.exp(re[j] - bb[SUB * j:SUB * (j + 1)]) for j in range(NSUB)], axis=0)
        pairs = [(i, j) for i in range(NSUB) for j in range(NSUB) if (j >= i if reverse else j <= i)]
        lhs = jnp.concatenate(
            [qh[i] * jnp.exp(jnp.minimum(rs[i] - re[j], EXP_CLAMP)) for (i, j) in pairs], axis=0).astype(BF16)
        kbd = jnp.where(bd, _tile_rows(kh.astype(BF16), 4), jnp.zeros((), BF16))
        prod = _dot_nt(lhs, kbd)
        a_tiles = []
        for i in range(NSUB):
            a_i = jnp.zeros((SUB, 256), F32)
            for p, (pi, pj) in enumerate(pairs):
                if pi == i:
                    a_i = jnp.where((sl8 >> 3) == pj, prod[SUB * p:SUB * (p + 1)], a_i)
            a_tiles.append(a_i)
        a_hg = jnp.where(causal256, jnp.concatenate(a_tiles, axis=0), 0.0).astype(BF16)
        vbd = jnp.where(bd, _tile_rows(v, 4), jnp.zeros((), BF16))
        o_hg = _dot(a_hg, vbd) + _dot_nt((q * jnp.exp(bb)).astype(BF16), s_hg[...].astype(BF16))
        u = _dot_tn(v, (key * jnp.exp(bend - bb)).astype(BF16))
        s_hg[...] = s_hg[...] * jnp.exp(bend) + jnp.where(bd, u, 0.0)
        if reverse:
            y_ref[rows, 0:256] = o_hg
        else:
            o_hg = o_hg + obwd_ref[rows, 0:256]
            ms = _seg_mean(o_hg * o_hg, ones_bd)
            gate = proj_ref[rows, C_HG + 1536:C_HG + 1792]
            y_ref[rows, 0:256] = (o_hg * lax.rsqrt(ms + EPS) * hgn_ref[...] * gate).astype(y_ref.dtype)

    sssd_out_ref[0] = s_ssd[0:64, :] + s_ssd[64:128, :]
    sret_out_ref[0] = s_ret[0:64, :] + s_ret[64:128, :] + s_ret[128:192, :] + s_ret[192:256, :]
    shg_out_ref[0] = s_hg[0:64, :] + s_hg[64:128, :] + s_hg[128:192, :] + s_hg[192:256, :]


def _block_table(seq_lens, reverse):
    rows = []
    blk0 = 0
    max_blocks = max(t // TB for t in seq_lens)
    for s, t in enumerate(seq_lens):
        nb = t // TB
        for k in range(nb):
            pos = (nb - 1 - k) if reverse else k
            rows.append([blk0 + pos, s, int(k == 0), int(pos == 0), int(pos == nb - 1), pos])
        blk0 += nb
    return np.asarray(rows, np.int32), blk0, max_blocks


def _tri_matrix(reverse):
    t = np.arange(TB)
    m = (t[None, :] >= t[:, None]) if reverse else (t[None, :] <= t[:, None])
    m = m & ((t[None, :] // CHUNK) == (t[:, None] // CHUNK))
    return jnp.asarray(m, BF16)


def _retention_constants(reverse):
    offset = RET_BWD_OFFSET if reverse else 0.0
    ld = np.log1p(-(2.0 ** (-5.0 - offset - np.arange(RET_HEADS, dtype=np.float64))))
    ld = np.repeat(ld, HEAD_DIM)[None, :]
    t = np.arange(CHUNK, dtype=np.float64)[:, None]
    s = (np.arange(256) % HEAD_DIM)[None, :].astype(np.float64)
    if reverse:
        b = (CHUNK - t) * ld
        dmat = np.where(s >= t, np.exp((s - t) * ld), 0.0)
    else:
        b = (t + 1.0) * ld
        dmat = np.where(s <= t, np.exp((t - s) * ld), 0.0)
    bend = CHUNK * ld
    f = lambda a: jnp.asarray(a, F32)
    return f(dmat), f(np.exp(b)), f(np.exp(bend - b)), f(np.exp(bend))


def _rope_tables(seq_lens, use_rope):
    n_pos = max(t for t, r in zip(seq_lens, use_rope) if r) if any(use_rope) else 0
    n_freq = HEAD_DIM // 4
    tok = np.arange(n_pos)
    rowp = (tok // GRID_W).astype(np.float32)
    colp = (tok % GRID_W).astype(np.float32)
    inv_freq = (ROPE_BASE ** (-np.arange(n_freq, dtype=np.float32) / n_freq)).astype(np.float32)
    ang = np.concatenate([rowp[:, None] * inv_freq, colp[:, None] * inv_freq], axis=-1)
    cos, sin = jnp.cos(jnp.asarray(ang)), jnp.sin(jnp.asarray(ang))
    cos_h = jnp.concatenate([cos, cos], axis=-1)
    sin_h = jnp.concatenate([-sin, sin], axis=-1)
    cos_t = jnp.concatenate([jnp.tile(cos_h, (1, RET_HEADS)), jnp.ones((TB, 256), F32)], axis=0)
    sin_t = jnp.concatenate([jnp.tile(sin_h, (1, RET_HEADS)), jnp.zeros((TB, 256), F32)], axis=0)
    return cos_t, sin_t, n_pos // TB


def mixer_direction(proj, edges, seq_lens, params, states, obwd, reverse):
    n_tok = proj.shape[0]
    n_seq = len(seq_lens)
    n_state = states[0].shape[0]
    tbl, n_blocks, _ = _block_table(seq_lens, reverse)
    tbl = np.concatenate([tbl[:, :6],
                          np.maximum(tbl[:, 0:1] - 1, 0),
                          np.minimum(tbl[:, 0:1] + 1, n_blocks - 1),
                          np.minimum(tbl[:, 1:2], n_state - 1),
                          (tbl[:, 1:2] < n_state).astype(np.int32)], axis=1).astype(np.int32)
    d = 1 if reverse else 0
    retd, reteq, retek, reteend = _retention_constants(reverse)
    rep = lambda a: jnp.repeat(a.astype(F32), HEAD_DIM)[None, :]
    expand = np.zeros((128, 512), np.float32)
    for h in range(SSD_HEADS):
        expand[8 * d + h, HEAD_DIM * h:HEAD_DIM * (h + 1)] = 1.0
    small = [
        _tri_matrix(reverse), jnp.asarray(expand, BF16),
        params["conv_w"].astype(F32),
        rep(params["d_skip"]),
        params["hg_norm_g"].astype(F32)[None, :], params["ssd_norm_g"].astype(F32)[None, :],
        params["ret_norm_g"].astype(F32)[None, :],
        retd, reteq, retek, reteend,
    ]
    full = lambda a: pl.BlockSpec(a.shape, lambda i, t: (0,) * a.ndim)
    in_specs = [
        pl.BlockSpec((TB, P_COLS), lambda i, t: (t[i, 0], 0)),
        pl.BlockSpec((8, 768), lambda i, t: (t[i, 6], 0)),
        pl.BlockSpec((8, 768), lambda i, t: (t[i, 7], 0)),
        pl.BlockSpec((8, 768), lambda i, t: (t[i, 0], 0)),
    ] + [full(a) for a in small] + [
        pl.BlockSpec((1, 64, 512), lambda i, t: (t[i, 8], 0, 0)),
        pl.BlockSpec((1, 64, 256), lambda i, t: (t[i, 8], 0, 0)),
        pl.BlockSpec((1, 64, 256), lambda i, t: (t[i, 8], 0, 0)),
    ]
    args = [proj, edges, edges, edges] + small + list(states)
    if not reverse:
        in_specs.append(pl.BlockSpec((TB, 1024), lambda i, t: (t[i, 0], 0)))
        args.append(obwd)
    out_dtype = F32 if reverse else BF16
    out_shape = [
        jax.ShapeDtypeStruct((n_tok, 1024), out_dtype),
        jax.ShapeDtypeStruct((n_seq, 64, 512), F32),
        jax.ShapeDtypeStruct((n_seq, 64, 256), F32),
        jax.ShapeDtypeStruct((n_seq, 64, 256), F32),
    ]
    out_specs = [
        pl.BlockSpec((TB, 1024), lambda i, t: (t[i, 0], 0)),
        pl.BlockSpec((1, 64, 512), lambda i, t: (t[i, 1], 0, 0)),
        pl.BlockSpec((1, 64, 256), lambda i, t: (t[i, 1], 0, 0)),
        pl.BlockSpec((1, 64, 256), lambda i, t: (t[i, 1], 0, 0)),
    ]
    scratch = [
        pltpu.VMEM((TB, 1280), F32),
        pltpu.VMEM((128, 512), F32),
        pltpu.VMEM((256, 256), F32),
        pltpu.VMEM((256, 256), F32),
    ]
    return pl.pallas_call(
        functools.partial(_mixer_kernel, reverse=reverse),
        grid_spec=pltpu.PrefetchScalarGridSpec(
            num_scalar_prefetch=1, grid=(n_blocks,), in_specs=in_specs, out_specs=out_specs,
            scratch_shapes=scratch),
        out_shape=out_shape,
        compiler_params=pltpu.CompilerParams(dimension_semantics=("arbitrary",), vmem_limit_bytes=48 * 1024 * 1024),
        name="mixer_bwd" if reverse else "mixer_fwd",
    )(jnp.asarray(tbl), *args)


ADA_TN = 1024


def _ada_kernel(c_ref, w_ref, b_ref, o_ref):
    s = _silu(c_ref[...]).astype(BF16)
    o_ref[0] = _dot(s, w_ref[0].astype(BF16)) + b_ref[0]


def ada_modulation(cond, w_ada, b_ada):
    n_c = cond.shape[0]
    n_l, d, n_out = w_ada.shape
    return pl.pallas_call(
        _ada_kernel,
        grid=(n_l, n_out // ADA_TN),
        in_specs=[pl.BlockSpec((n_c, d), lambda l, j: (0, 0)),
                  pl.BlockSpec((1, d, ADA_TN), lambda l, j: (l, 0, j)),
                  pl.BlockSpec((1, 1, ADA_TN), lambda l, j: (l, 0, j))],
        out_specs=pl.BlockSpec((1, n_c, ADA_TN), lambda l, j: (l, 0, j)),
        out_shape=jax.ShapeDtypeStruct((n_l, n_c, n_out), F32),
        name="ada_mod",
    )(cond, w_ada, b_ada.reshape(n_l, 1, n_out))


TM = 256


def _rms(x):
    return x * lax.rsqrt(jnp.mean(x * x, axis=-1, keepdims=True) + EPS)


def _load_tokens(ref, n_tok):
    return jnp.concatenate([ref[pl.ds(c, n_tok, stride=8), :] for c in range(8)], axis=1)


def _store_tokens(ref, val, n_tok):
    for c in range(8):
        ref[pl.ds(c, n_tok, stride=8), :] = val[:, 128 * c:128 * (c + 1)]


def _rope(a, cos, sin):
    half_lo = (lax.broadcasted_iota(jnp.int32, a.shape, 1) & 63) < 32
    ar = jnp.where(half_lo, pltpu.roll(a, 224, axis=1), pltpu.roll(a, 32, axis=1))
    return a * cos + ar * sin


def _inproj_kernel(crow_ref, ropeb_ref, x_ref, mod_ref, g_ref, w_ref, dtb_ref, alog_ref, lb_ref, cw_ref, cb_ref,
                   cos_ref, sin_ref, o_ref, e_ref, *, tiles):
    r = crow_ref[pl.program_id(0)]
    sh = mod_ref[pl.ds(r, 1), 0:1024]
    sc = mod_ref[pl.ds(r, 1), 1024:2048]
    x = _load_tokens(x_ref, TM) if tiles else x_ref[...]
    h = _rms(x) * g_ref[...] * (1.0 + sc) + sh
    raw_all = _dot(h.astype(BF16), w_ref[...])

    def raw(col, width):
        return raw_all[:, col:col + width]

    cos = cos_ref[...]
    sin = sin_ref[...]
    o_ref[:, C_RET:C_RET + 256] = _rope(raw(R_RET, 256), cos, sin)
    o_ref[:, C_RET + 256:C_RET + 512] = _rope(raw(R_RET + 256, 256) * (HEAD_DIM ** -0.5), cos, sin)
    o_ref[:, C_RET + 512:C_RET + 768] = raw(R_RET + 512, 256)
    o_ref[:, C_RET + 768:C_RET + 1024] = _silu(raw(R_RET + 768, 256))
    o_ref[:, C_HG:C_HG + 256] = _silu(raw(R_HG, 256)) * (HEAD_DIM ** -0.5)
    o_ref[:, C_HG + 256:C_HG + 512] = raw(R_HG + 256, 256)
    for dirn in range(2):
        lb = lb_ref[dirn:dirn + 1, :]
        fr = raw(R_HG + 512 + 256 * dirn, 256)
        o_ref[:, C_HG + 512 + 256 * dirn:C_HG + 768 + 256 * dirn] = jnp.log(lb + (1.0 - lb) * _sigmoid(fr))
        o_ref[:, C_HG + 1024 + 256 * dirn:C_HG + 1280 + 256 * dirn] = (1.0 - lb) * _sigmoid(-fr)
    o_ref[:, C_HG + 1536:C_HG + 1792] = _silu(raw(R_HG + 1024, 256))
    o_ref[:, C_Z:C_Z + 512] = _silu(raw(R_Z, 512))
    dt = _softplus(raw(R_DT, 128) + dtb_ref[...])
    o_ref[:, C_DT:C_DT + 128] = dt
    o_ref[:, C_LA:C_LA + 128] = -dt * jnp.exp(alog_ref[...])

    xbc = raw(R_XBC, 768)
    row = lax.broadcasted_iota(jnp.int32, (CHUNK, 768), 0)
    zero_row = jnp.zeros((1, 768), F32)
    for s in range(TM // CHUNK):
        r0 = s * CHUNK
        xb = xbc[r0:r0 + CHUNK, :]
        before = xbc[r0 - 1:r0, :] if s > 0 else zero_row
        after = xbc[r0 + CHUNK:r0 + CHUNK + 1, :] if s < TM // CHUNK - 1 else zero_row
        xm1 = jnp.where(row == 0, before, pltpu.roll(xb, 1, axis=0))
        xp1 = jnp.where(row == CHUNK - 1, after, pltpu.roll(xb, CHUNK - 1, axis=0))
        pre = cw_ref[0:1, :] * xm1 + cw_ref[1:2, :] * xb + cw_ref[2:3, :] * xp1 + cb_ref[...]
        o_ref[r0:r0 + CHUNK, C_XBC:C_XBC + 768] = _silu(pre)
        if s == 0:
            e_ref[0:1, :] = xb[0:1, :]
            e_ref[2:3, :] = pre[0:1, :]
        if s == TM // CHUNK - 1:
            e_ref[1:2, :] = xb[CHUNK - 1:CHUNK, :]
            e_ref[3:4, :] = pre[CHUNK - 1:CHUNK, :]
    e_ref[4:8, :] = jnp.zeros((4, 768), F32)


def in_projection(x8, crow, rope_blk, mod, gain, w, dt_bias, a_log, lb, conv_w, conv_b, cos_t, sin_t):
    tiles = x8.shape[1] == 128
    n_tok, d = (x8.shape[0] // 8 if tiles else x8.shape[0]), D_MODEL
    compact = lambda a: jnp.zeros((1, 128), F32).at[0, :2 * SSD_HEADS].set(a.astype(F32).reshape(-1))
    const = lambda shape: pl.BlockSpec(shape, lambda i, c, rb: (0, 0))
    return pl.pallas_call(
        functools.partial(_inproj_kernel, tiles=tiles),
        grid_spec=pltpu.PrefetchScalarGridSpec(
            num_scalar_prefetch=2, grid=(n_tok // TM,),
            in_specs=[pl.BlockSpec((TM * 8, 128) if tiles else (TM, d), lambda i, c, rb: (i, 0)),
                      const(mod.shape), const((1, d)), const((d, R_COLS)), const((1, 128)), const((1, 128)),
                      const((2, HG_W)), const((3, SSD_CONV_CH)), const((1, SSD_CONV_CH)),
                      pl.BlockSpec((TM, 256), lambda i, c, rb: (rb[i], 0)),
                      pl.BlockSpec((TM, 256), lambda i, c, rb: (rb[i], 0))],
            out_specs=[pl.BlockSpec((TM, P_COLS), lambda i, c, rb: (i, 0)),
                       pl.BlockSpec((8, SSD_CONV_CH), lambda i, c, rb: (i, 0))]),
        out_shape=[jax.ShapeDtypeStruct((n_tok, P_COLS), F32),
                   jax.ShapeDtypeStruct((n_tok // TM * 8, SSD_CONV_CH), F32)],
        compiler_params=pltpu.CompilerParams(dimension_semantics=("arbitrary",), vmem_limit_bytes=48 * 1024 * 1024),
        name="in_proj",
    )(crow, rope_blk, x8, mod, gain[None, :], w, compact(dt_bias), compact(a_log), lb.astype(F32),
      conv_w.astype(F32), conv_b.astype(F32)[None, :], cos_t, sin_t)


def _outproj_kernel(crow_ref, y_ref, x_ref, mod_ref, g_ref, wo_ref, wr_ref, x1_ref, aff_ref, *, tiles):
    r = crow_ref[pl.program_id(0)]
    g1 = mod_ref[pl.ds(r, 1), 2048:3072]
    sh = mod_ref[pl.ds(r, 1), 3072:4096]
    sc = mod_ref[pl.ds(r, 1), 4096:5120]
    x = _load_tokens(x_ref, TM) if tiles else x_ref[...]
    x1 = x + g1 * _dot(y_ref[...], wo_ref[...])
    _store_tokens(x1_ref, x1, TM)
    h = _rms(x1) * g_ref[...] * (1.0 + sc) + sh
    lt = _dot_nt(wr_ref[...], h.astype(BF16))
    e = jnp.exp(lt - jnp.max(lt, axis=0, keepdims=True))
    aff_ref[...] = e / jnp.sum(e, axis=0, keepdims=True)


def out_projection(y, x8, crow, mod, gain, w_out, w_router):
    tiles = x8.shape[1] == 128
    n_tok, d = (x8.shape[0] // 8 if tiles else x8.shape[0]), D_MODEL
    wr = w_router.astype(BF16).T
    return pl.pallas_call(
        functools.partial(_outproj_kernel, tiles=tiles),
        grid_spec=pltpu.PrefetchScalarGridSpec(
            num_scalar_prefetch=1, grid=(n_tok // TM,),
            in_specs=[pl.BlockSpec((TM, d), lambda i, c: (i, 0)),
                      pl.BlockSpec((TM * 8, 128) if tiles else (TM, d), lambda i, c: (i, 0)),
                      pl.BlockSpec(mod.shape, lambda i, c: (0, 0)),
                      pl.BlockSpec((1, d), lambda i, c: (0, 0)),
                      pl.BlockSpec((d, d), lambda i, c: (0, 0)),
                      pl.BlockSpec((N_EXPERTS, d), lambda i, c: (0, 0))],
            out_specs=[pl.BlockSpec((TM * 8, 128), lambda i, c: (i, 0)),
                       pl.BlockSpec((N_EXPERTS, TM), lambda i, c: (0, i))]),
        out_shape=[jax.ShapeDtypeStruct((n_tok * 8, 128), F32), jax.ShapeDtypeStruct((N_EXPERTS, n_tok), F32)],
        compiler_params=pltpu.CompilerParams(dimension_semantics=("arbitrary",)),
        name="out_proj",
    )(crow, y, x8, mod, gain[None, :], w_out, wr)


TK_BLK = 256
TK_JB = 256


def _topk_kernel(aff_ref, tri_ref, idx_ref, gate_ref, csum_scr, sel_scr, ti_scr, tg_scr, *, n, cap, cpad, npair):
    a = aff_ref[...]

    def count_ge(t):
        return jnp.sum((a >= t).astype(jnp.int32), axis=1, keepdims=True)

    def search(it, lo):
        cand = lo | (1 << (30 - it))
        return jnp.where(count_ge(pltpu.bitcast(cand, F32)) >= cap, cand, lo)

    lo_bits = lax.fori_loop(0, 31, search, jnp.zeros((npair, 1), jnp.int32))

    def refine(it, lh):
        lo, hi = lh
        mid = 0.5 * (lo + hi)
        ok = count_ge(mid) >= cap
        return jnp.where(ok, mid, lo), jnp.where(ok, hi, mid)

    lo_v, hi_v = lax.fori_loop(0, 32, refine, (pltpu.bitcast(lo_bits, F32), pltpu.bitcast(lo_bits + 1, F32)))
    gt = a >= hi_v
    eq = (a >= lo_v) & (a < hi_v)
    need = (cap - jnp.sum(gt.astype(jnp.int32), axis=1, keepdims=True)).astype(F32)

    tri = tri_ref[...]

    def prefix(mask):
        carry = jnp.zeros((npair, 1), F32)
        parts = []
        for kb in range(n // TK_BLK):
            blk = mask[:, kb * TK_BLK:(kb + 1) * TK_BLK].astype(BF16)
            loc = _dot(blk, tri) + carry
            parts.append(loc)
            carry = loc[:, TK_BLK - 1:TK_BLK]
        return jnp.concatenate(parts, axis=1)

    sel = gt | (eq & (prefix(eq) <= need))
    csum_scr[...] = prefix(sel)
    sel_scr[...] = jnp.where(sel, a, 0.0)
    tile_j = min(TK_JB, cpad)
    lane =lax.broadcasted_iota(jnp.int32, (tile_j, max(npair, 128)), 1)

    def per_expert(e, carry):
        cs = csum_scr[pl.ds(e, 1), :]
        av = sel_scr[pl.ds(e, 1), :]

        def per_block(jb, c2):
            j0 = pl.multiple_of(jb * tile_j, tile_j)
            j = (lax.broadcasted_iota(jnp.int32, (tile_j, n), 0) + j0).astype(F32)
            idx_col = jnp.sum((cs <= j).astype(F32), axis=1, keepdims=True)
            gate_col = jnp.sum(jnp.where(cs == j + 1.0, av, 0.0), axis=1, keepdims=True)
            rows = pl.ds(j0, tile_j)
            ti_scr[rows, :] = jnp.where(lane == e, idx_col, ti_scr[rows, :])
            tg_scr[rows, :] = jnp.where(lane == e, gate_col, tg_scr[rows, :])
            return c2

        return lax.fori_loop(0, cpad // tile_j, per_block, carry)

    lax.fori_loop(0, npair, per_expert, 0, unroll=4 if cpad // tile_j == 1 else 1)
    idx_ref[...] = ti_scr[...].T[0:npair, :].astype(jnp.int32)
    gate_ref[...] = tg_scr[...].T[0:npair, :]


def expert_choice_topk(aff_t, tok0, n_rows, n, rows_per_step):
    cap = EC_CAPACITY_FACTOR * n // N_EXPERTS
    cpad = max(cap, 128)
    npair = rows_per_step * N_EXPERTS
    plane = max(npair, 128)
    pairs = aff_t[:, tok0:tok0 + n_rows * n].reshape(N_EXPERTS, n_rows, n).transpose(1, 0, 2).reshape(n_rows * N_EXPERTS, n)
    t = np.arange(TK_BLK)
    tri = jnp.asarray(t[:, None] <= t[None, :], BF16)
    idx, gate = pl.pallas_call(
        functools.partial(_topk_kernel, n=n, cap=cap, cpad=cpad, npair=npair),
        grid=(n_rows // rows_per_step,),
        in_specs=[pl.BlockSpec((npair, n), lambda r: (r, 0)),
                  pl.BlockSpec((TK_BLK, TK_BLK), lambda r: (0, 0))],
        out_specs=[pl.BlockSpec((npair, cpad), lambda r: (r, 0)),
                   pl.BlockSpec((npair, cpad), lambda r: (r, 0))],
        out_shape=[jax.ShapeDtypeStruct((n_rows * N_EXPERTS, cpad), jnp.int32),
                   jax.ShapeDtypeStruct((n_rows * N_EXPERTS, cpad), F32)],
        scratch_shapes=[pltpu.VMEM((npair, n), F32), pltpu.VMEM((npair, n), F32),
                        pltpu.VMEM((cpad, plane), F32), pltpu.VMEM((cpad, plane), F32)],
        compiler_params=pltpu.CompilerParams(dimension_semantics=("arbitrary",)),
        name=f"topk_{n}",
    )(pairs, tri)
    return (idx[:, :cap].reshape(n_rows, N_EXPERTS, cap), gate[:, :cap].reshape(n_rows, N_EXPERTS, cap))


MOE_NT = 4096
MOE_C = 512
MOE_GROUP = 8


def _moe_kernel(crow_ref, idx_ref, gate_ref, x_ref, mod_ref, g_ref, wg_ref, wu_ref, wd_ref, o_ref, xs_scr, ys_scr):
    grp = pl.program_id(0)
    e = pl.program_id(1)
    r = crow_ref[grp]
    base = (grp * N_EXPERTS + e) * MOE_C
    x8 = x_ref.at[0]
    o8 = o_ref.at[0]

    def tile(i):
        return pl.ds(pl.multiple_of(i * 8, 8), 8)

    def tok_tile(p):
        return pl.ds(pl.multiple_of(idx_ref[p], 8), 8)

    @pl.when(e == 0)
    def _():
        o_ref[...] = x_ref[...]

    def gather(j, c):
        xs_scr[tile(j), :] = x8[tok_tile(base + j), :]
        return c

    lax.fori_loop(0, MOE_C, gather, 0, unroll=16)

    sh = mod_ref[pl.ds(r, 1), 3072:4096]
    sc = mod_ref[pl.ds(r, 1), 4096:5120]
    g2 = mod_ref[pl.ds(r, 1), 5120:6144]
    h = (_rms(_load_tokens(xs_scr, MOE_C)) * g_ref[...] * (1.0 + sc) + sh).astype(BF16)
    hid = _silu(_dot(h, wg_ref[0].astype(BF16))) * _dot(h, wu_ref[0].astype(BF16))
    gate_col = jnp.broadcast_to(gate_ref[0], (128, MOE_C)).T[:, 0:1]
    _store_tokens(ys_scr, _dot(hid.astype(BF16), wd_ref[0].astype(BF16)) * g2 * gate_col, MOE_C)

    def scatter(grp_j, c):
        j0 = grp_j * MOE_GROUP
        toks = [tok_tile(base + j0 + k) for k in range(MOE_GROUP)]
        vals = [o8[toks[k], :] + ys_scr[tile(j0 + k), :] for k in range(MOE_GROUP)]
        for k in range(MOE_GROUP):
            o8[toks[k], :] = vals[k]
        return c

    lax.fori_loop(0, MOE_C // MOE_GROUP, scatter, 0)


def moe_ffn(x8, idx, gate, crow_grp, mod, gain, w_gate, w_up, w_down, layer):
    n_tok, d = x8.shape[0] // 8, D_MODEL
    n_grp = n_tok // MOE_NT
    xg = x8.reshape(n_grp, MOE_NT * 8, 128)
    out = pl.pallas_call(
        _moe_kernel,
        grid_spec=pltpu.PrefetchScalarGridSpec(
            num_scalar_prefetch=2, grid=(n_grp, N_EXPERTS),
            in_specs=[pl.BlockSpec((1, 1, MOE_C), lambda g, e, c, ix: (g * N_EXPERTS + e, 0, 0)),
                      pl.BlockSpec((1, MOE_NT * 8, 128), lambda g, e, c, ix: (g, 0, 0), pipeline_mode=pl.Buffered(1)),
                      pl.BlockSpec(mod.shape, lambda g, e, c, ix: (0, 0)),
                      pl.BlockSpec((1, d), lambda g, e, c, ix: (0, 0)),
                      pl.BlockSpec((None, 1, d, D_EXPERT), lambda g, e, c, ix: (layer, e, 0, 0)),
                      pl.BlockSpec((None, 1, d, D_EXPERT), lambda g, e, c, ix: (layer, e, 0, 0)),
                      pl.BlockSpec((None, 1, D_EXPERT, d), lambda g, e, c, ix: (layer, e, 0, 0))],
            out_specs=pl.BlockSpec((1, MOE_NT * 8, 128), lambda g, e, c, ix: (g, 0, 0), pipeline_mode=pl.Buffered(1)),
            scratch_shapes=[pltpu.VMEM((MOE_C * 8, 128), F32), pltpu.VMEM((MOE_C * 8, 128), F32)]),
        out_shape=jax.ShapeDtypeStruct((n_grp, MOE_NT * 8, 128), F32),
        compiler_params=pltpu.CompilerParams(dimension_semantics=("arbitrary", "arbitrary"),
                                             vmem_limit_bytes=56 * 1024 * 1024),
        name="moe_ffn",
    )(crow_grp, idx, gate, xg, mod, gain[None, :], w_gate, w_up, w_down)
    return out.reshape(n_tok * 8, 128)


def _final_kernel(x_ref, g_ref, o_ref):
    o_ref[...] = _rms(_load_tokens(x_ref, TM)) * g_ref[...]


def final_norm(x8, gain, tok0, n_out):
    blk0 = tok0 // TM
    return pl.pallas_call(
        _final_kernel,
        grid=(n_out // TM,),
        in_specs=[pl.BlockSpec((TM * 8, 128), lambda i: (blk0 + i, 0)),
                  pl.BlockSpec((1, D_MODEL), lambda i: (0, 0))],
        out_specs=pl.BlockSpec((TM, D_MODEL), lambda i: (i, 0)),
        out_shape=jax.ShapeDtypeStruct((n_out, D_MODEL), F32),
        name="final_norm",
    )(x8, gain[None, :])


def _ssd_to_k(s):
    return s.transpose(0, 2, 1, 3).reshape(s.shape[0], 64, 512)


def _ssd_from_k(m):
    return m.reshape(m.shape[0], 64, 8, 64).transpose(0, 2, 1, 3)


def _ret_to_k(s):
    return s.transpose(0, 2, 1, 3).reshape(s.shape[0], 64, 256)


def _ret_from_k(m):
    return m.reshape(m.shape[0], 64, 4, 64).transpose(0, 2, 1, 3)


def _hg_to_k(s):
    return s.transpose(0, 3, 1, 2).reshape(s.shape[0], 64, 256)


def _hg_from_k(m):
    return m.reshape(m.shape[0], 64, 4, 64).transpose(0, 2, 3, 1)


def _arrange_w_in(w):
    hg = w[:, 0:1280]
    z = w[:, 1280:1792]
    xbc = w[:, 1792:2560]
    dt = jnp.pad(w[:, 2560:2576], ((0, 0), (0, 128 - 2 * SSD_HEADS)))
    ret = w[:, 2576:3600]
    return jnp.concatenate([xbc, z, dt, hg, ret], axis=1).astype(BF16)


def kernel(x_prompt, x_sample, state_hgrn, state_ssd, state_ret, c, c_ctx, norm1_g, norm2_g, final_g,
           w_ada, b_ada, w_in, w_out, hg_lb, hg_norm_g, ssd_conv_w, ssd_conv_b, ssd_dt_bias, ssd_a_log,
           ssd_d, ssd_norm_g, ret_norm_g, w_router, w_gate, w_up, w_down):
    n_ctx, t_ctx, d = x_prompt.shape
    n_smp, t_smp, _ = x_sample.shape
    n_tok_s = n_smp * t_smp
    n_tok = n_tok_s + n_ctx * t_ctx
    seq_lens = [t_smp] * n_smp + [t_ctx] * n_ctx
    use_rope = [True] * n_smp + [False] * n_ctx

    x = jnp.concatenate([x_sample.reshape(n_tok_s, d), x_prompt.reshape(n_ctx * t_ctx, d)], axis=0)
    cond = jnp.zeros((16, d), F32).at[0].set(c_ctx).at[1:1 + n_smp].set(c)
    crow_tm = jnp.asarray([i * TM // t_smp + 1 if i * TM < n_tok_s else 0 for i in range(n_tok // TM)], jnp.int32)
    crow_grp = jnp.asarray([g + 1 if g < n_smp else 0 for g in range(n_tok // MOE_NT)], jnp.int32)
    cos_t, sin_t, ident_blk = _rope_tables(seq_lens, use_rope)
    rope_blk = jnp.asarray([i % (t_smp // TM) if i * TM < n_tok_s else ident_blk for i in range(n_tok // TM)], jnp.int32)

    lb_p = jax.nn.softmax(hg_lb.astype(F32), axis=0)
    lb_all = jnp.cumsum(lb_p, axis=0) - lb_p[0]
    mod_all = ada_modulation(cond, w_ada, b_ada)

    rows_per_grp = MOE_NT // t_ctx
    n_grp_ctx = n_ctx // rows_per_grp
    new_states = {"hg": [], "ssd": [], "ret": []}
    for l in range(DEPTH):
        mod = mod_all[l]
        proj, edges = in_projection(x, crow_tm, rope_blk, mod, norm1_g[l], _arrange_w_in(w_in[l]), ssd_dt_bias[l],
                                    ssd_a_log[l], lb_all[l], ssd_conv_w[l], ssd_conv_b[l], cos_t, sin_t)
        params = dict(conv_w=ssd_conv_w[l], d_skip=ssd_d[l], hg_norm_g=hg_norm_g[l],
                      ssd_norm_g=ssd_norm_g[l], ret_norm_g=ret_norm_g[l])
        outs = []
        for dirn in (1, 0):
            st = (_ssd_to_k(state_ssd[:, l, dirn]), _ret_to_k(state_ret[:, l, dirn]), _hg_to_k(state_hgrn[:, l, dirn]))
            outs.append(mixer_direction(proj, edges, seq_lens, params, st, outs[0][0] if outs else None, dirn == 1))
        (_, b_ssd, b_ret, b_hg), (y, f_ssd, f_ret, f_hg) = outs
        new_states["ssd"].append(jnp.stack([_ssd_from_k(f_ssd[n_smp:]), _ssd_from_k(b_ssd[n_smp:])], axis=1))
        new_states["ret"].append(jnp.stack([_ret_from_k(f_ret[n_smp:]), _ret_from_k(b_ret[n_smp:])], axis=1))
        new_states["hg"].append(jnp.stack([_hg_from_k(f_hg[n_smp:]), _hg_from_k(b_hg[n_smp:])], axis=1))

        x1, aff_t = out_projection(y, x, crow_tm, mod, norm2_g[l], w_out[l].astype(BF16), w_router[l])
        idx_s, gate_s = expert_choice_topk(aff_t, 0, n_smp, t_smp, 1)
        idx_c, gate_c = expert_choice_topk(aff_t, n_tok_s, n_ctx, t_ctx, rows_per_grp)
        cap_c = idx_c.shape[-1]
        idx_c = idx_c + (jnp.arange(n_ctx, dtype=jnp.int32) % rows_per_grp)[:, None, None] * t_ctx
        regroup = lambda a: a.reshape(n_grp_ctx, rows_per_grp, N_EXPERTS, cap_c).transpose(0, 2, 1, 3).reshape(
            n_grp_ctx, N_EXPERTS, rows_per_grp * cap_c)
        idx = jnp.concatenate([idx_s, regroup(idx_c)], axis=0).reshape(-1) * 8
        gate = jnp.concatenate([gate_s, regroup(gate_c)], axis=0).reshape(-1, 1, MOE_C)
        x = moe_ffn(x1, idx, gate, crow_grp, mod, norm2_g[l], w_gate, w_up, w_down, l)

    y_sample = final_norm(x, final_g, 0, n_tok_s).reshape(n_smp, t_smp, d)
    y_prompt = final_norm(x, final_g, n_tok_s, n_ctx * t_ctx).reshape(n_ctx, t_ctx, d)
    return (y_prompt, y_sample, jnp.stack(new_states["hg"], axis=1), jnp.stack(new_states["ssd"], axis=1),
            jnp.stack(new_states["ret"], axis=1))
```

```python
import functools
import math

import jax
import jax.numpy as jnp
import numpy as np
from jax import lax
from jax.experimental import pallas as pl
from jax.experimental.pallas import tpu as pltpu

D_MODEL = 1024
DEPTH = 2
GRID_W = 64
CHUNK = 64
EPS = 1e-6
HEAD_DIM = 64
HG_W = 256
SSD_W = 512
RET_W = 256
HG_HEADS = 4
SSD_HEADS = 8
SSD_N = 64
SSD_GROUPS = 2
SSD_CONV_CH = 768
RET_HEADS = 4
RET_BWD_OFFSET = 0.5
ROPE_BASE = 10000.0
N_EXPERTS = 16
EC_CAPACITY_FACTOR = 2
D_EXPERT = 512

R_XBC = 0
R_Z = 768
R_DT = 1280
R_HG = 1408
R_RET = 2688
R_COLS = 3712

C_XBC = 0
C_Z = 768
C_DT = 1280
C_LA = 1408
C_HG = 1536
C_RET = 3328
P_COLS = 4352

TB = 256
NCH = TB // CHUNK
SUB = 8
NSUB = CHUNK // SUB
EXP_CLAMP = 80.0

F32 = jnp.float32
BF16 = jnp.bfloat16


def _sigmoid(x):
    return 1.0 / (1.0 + jnp.exp(-x))


def _silu(x):
    return x * (0.5 + 0.5 * jnp.tanh(0.5 * x))


def _softplus(x):
    return jnp.maximum(x, 0.0) + jnp.log1p(jnp.exp(-jnp.abs(x)))


def _dot(a, b):
    return jnp.dot(a, b, preferred_element_type=F32)


def _dot_nt(a, b):
    return lax.dot_general(a, b, (((1,), (1,)), ((), ())), preferred_element_type=F32)


def _dot_tn(a, b):
    return lax.dot_general(a, b, (((0,), (0,)), ((), ())), preferred_element_type=F32)


def _split3(x):
    hi = x.astype(BF16)
    r = x - hi.astype(F32)
    mid = r.astype(BF16)
    lo = (r - mid.astype(F32)).astype(BF16)
    return hi, mid, lo


def _seg_mean(x, ones_bd):
    hi = x.astype(BF16)
    lo = (x - hi.astype(F32)).astype(BF16)
    return (_dot(hi, ones_bd) + _dot(lo, ones_bd)) * (1.0 / HEAD_DIM)


def _tile_rows(x, n):
    return jnp.concatenate([x] * n, axis=0)


def _mixer_kernel(tbl_ref, proj_ref, prev_ref, next_ref, own_ref, tri_ref, expand_ref,
                  convw_ref, dskip_ref,
                  hgn_ref, ssdn_ref, retn_ref, retd_ref, reteq_ref, retek_ref, reteend_ref,
                  sssd_in_ref, sret_in_ref, shg_in_ref, *rest, reverse):
    if reverse:
        (y_ref, sssd_out_ref, sret_out_ref, shg_out_ref, b_scr, s_ssd, s_ret, s_hg) = rest
        obwd_ref = None
    else:
        (obwd_ref, y_ref, sssd_out_ref, sret_out_ref, shg_out_ref, b_scr, s_ssd, s_ret, s_hg) = rest

    step = pl.program_id(0)
    is_first = tbl_ref[step, 2]
    keep_prev = (1 - tbl_ref[step, 3]).astype(F32)
    keep_next = (1 - tbl_ref[step, 4]).astype(F32)

    r256 = lax.broadcasted_iota(jnp.int32, (256, 256), 0)
    c256 = lax.broadcasted_iota(jnp.int32, (256, 256), 1)
    bd = (r256 >> 6) == (c256 >> 6)
    gmask_s = (lax.broadcasted_iota(jnp.int32, (128, 512), 0) >> 6) == (lax.broadcasted_iota(jnp.int32, (128, 512), 1) >> 8)

    @pl.when(is_first == 1)
    def _():
        has = tbl_ref[step, 9].astype(F32)
        s_ssd[...] = jnp.where(gmask_s, _tile_rows(sssd_in_ref[0], 2), 0.0) * has
        s_ret[...] = jnp.where(bd, _tile_rows(sret_in_ref[0], 4), 0.0) * has
        s_hg[...] = jnp.where(bd, _tile_rows(shg_in_ref[0], 4), 0.0) * has

    lf_col = C_HG + (768 if reverse else 512)
    key_col = C_HG + (1280 if reverse else 1024)

    xbc_first = _silu(own_ref[2:3, :] + convw_ref[0:1, :] * prev_ref[1:2, :] * keep_prev)
    xbc_last = _silu(own_ref[3:4, :] + convw_ref[2:3, :] * next_ref[0:1, :] * keep_next)

    def split2(a):
        hi = a.astype(BF16)
        return hi, (a - hi.astype(F32)).astype(BF16)

    la_hi, la_lo = split2(proj_ref[:, C_LA:C_LA + 128])
    lf_hi, lf_lo = split2(proj_ref[:, lf_col:lf_col + 256])
    cum = _dot(tri_ref[...], jnp.concatenate([la_hi, la_lo, lf_hi, lf_lo], axis=1))
    b_scr[:, 512:768] = cum[:, 256:512] + cum[:, 512:768]
    c_hi, c_lo = split2(cum[:, 0:128] + cum[:, 128:256])
    d_hi, d_lo = split2(proj_ref[:, C_DT:C_DT + 128])
    spread = _dot(jnp.concatenate([c_hi, c_lo, d_hi, d_lo], axis=0), expand_ref[...])
    b_scr[:, 0:512] = spread[0:TB] + spread[TB:2 * TB]
    b_scr[:, 768:1280] = spread[2 * TB:3 * TB] + spread[3 * TB:4 * TB]

    gmask_b = (lax.broadcasted_iota(jnp.int32, (512, 128), 0) >> 8) == (lax.broadcasted_iota(jnp.int32, (512, 128), 1) >> 6)
    t512 = lax.broadcasted_iota(jnp.int32, (CHUNK, 512), 0)
    s512 = lax.broadcasted_iota(jnp.int32, (CHUNK, 512), 1) & 63
    t256 = lax.broadcasted_iota(jnp.int32, (CHUNK, 256), 0)
    s256 = lax.broadcasted_iota(jnp.int32, (CHUNK, 256), 1) & 63
    if reverse:
        causal512, causal256 = s512 >= t512, s256 >= t256
    else:
        causal512, causal256 = s512 <= t512, s256 <= t256
    diag512 = s512 == t512
    sl8 = lax.broadcasted_iota(jnp.int32, (SUB, 256), 1) & 63
    row768 = lax.broadcasted_iota(jnp.int32, (CHUNK, 768), 0)
    ones_bd = bd.astype(BF16)
    end_row = 0 if reverse else CHUNK - 1

    for k in range(NCH):
        c = (NCH - 1 - k) if reverse else k
        r0 = c * CHUNK
        rows = pl.ds(r0, CHUNK)

        xbc = proj_ref[rows, C_XBC:C_XBC + 768]
        if c == 0:
            xbc = jnp.where(row768 == 0, xbc_first, xbc)
        if c == NCH - 1:
            xbc = jnp.where(row768 == CHUNK - 1, xbc_last, xbc)
        bb = b_scr[rows, 0:512]
        dtc = b_scr[rows, 768:1280]
        xs = xbc[:, 0:512]
        bm = xbc[:, 512:640].astype(BF16)
        cm = xbc[:, 640:768].astype(BF16)
        xdt = xs * dtc
        bend = b_scr[pl.ds(r0 + end_row, 1), 0:512]
        rvec = jnp.sum(jnp.where(diag512, bb, 0.0), axis=0, keepdims=True)
        dmat = jnp.where(causal512, jnp.exp(jnp.minimum(bb - rvec, 0.0)), 0.0)
        rhs_b = jnp.where(gmask_b, _tile_rows(bm, 8), jnp.zeros((), BF16))
        a_ssd = (_dot_nt(cm, rhs_b) * dmat).astype(BF16)
        xdt_b = xdt.astype(BF16)
        o_parts = []
        for g in range(SSD_GROUPS):
            vbd = jnp.where(bd, _tile_rows(xdt_b[:, g * 256:(g + 1) * 256], 4), jnp.zeros((), BF16))
            o_parts.append(_dot(a_ssd[:, g * 256:(g + 1) * 256], vbd))
        o_ssd = jnp.concatenate(o_parts, axis=1) + jnp.exp(bb) * _dot(cm, s_ssd[...].astype(BF16))
        u = _dot_tn(bm, (xdt * jnp.exp(bend - bb)).astype(BF16))
        s_ssd[...] = s_ssd[...] * jnp.exp(bend) + jnp.where(gmask_s, u, 0.0)
        if reverse:
            y_ref[rows, 256:768] = o_ssd
        else:
            y = o_ssd + obwd_ref[rows, 256:768] + xs * dskip_ref[...]
            y = y * proj_ref[rows, C_Z:C_Z + 512]
            y = y * lax.rsqrt(jnp.mean(y * y, axis=-1, keepdims=True) + EPS) * ssdn_ref[...]
            y_ref[rows, 256:768] = y.astype(y_ref.dtype)

        q = proj_ref[rows, C_RET:C_RET + 256]
        kk = proj_ref[rows, C_RET + 256:C_RET + 512]
        v = proj_ref[rows, C_RET + 512:C_RET + 768].astype(BF16)
        kbd = jnp.where(bd, _tile_rows(kk.astype(BF16), 4), jnp.zeros((), BF16))
        a_ret = (_dot_nt(q.astype(BF16), kbd) * retd_ref[...]).astype(BF16)
        vbd = jnp.where(bd, _tile_rows(v, 4), jnp.zeros((), BF16))
        o_ret = _dot(a_ret, vbd) + _dot((q * reteq_ref[...]).astype(BF16), s_ret[...].astype(BF16))
        u = _dot_tn((kk * retek_ref[...]).astype(BF16), v)
        s_ret[...] = s_ret[...] * reteend_ref[...] + jnp.where(bd, u, 0.0)
        if reverse:
            y_ref[rows, 768:1024] = o_ret
        else:
            o_ret = o_ret + obwd_ref[rows, 768:1024]
            xc = o_ret - _seg_mean(o_ret, ones_bd)
            var = _seg_mean(xc * xc, ones_bd)
            gate = proj_ref[rows, C_RET + 768:C_RET + 1024]
            y_ref[rows, 768:1024] = (xc * lax.rsqrt(var + EPS) * retn_ref[...] * gate).astype(y_ref.dtype)

        bb = b_scr[rows, 512:768]
        q = proj_ref[rows, C_HG:C_HG + 256]
        v = proj_ref[rows, C_HG + 256:C_HG + 512].astype(BF16)
        key = proj_ref[rows, key_col:key_col + 256]
        bend = b_scr[pl.ds(r0 + end_row, 1), 512:768]

        def brow(r):
            return b_scr[pl.ds(r0 + r, 1), 512:768]

        zero_row = jnp.zeros((1, 256), F32)
        if reverse:
            rs = [brow(SUB * i + SUB) if i < NSUB - 1 else zero_row for i in range(NSUB)]
            re = [brow(SUB * j) for j in range(NSUB)]
        else:
            rs = [brow(SUB * i - 1) if i > 0 else zero_row for i in range(NSUB)]
            re = [brow(SUB * j + SUB - 1) for j in range(NSUB)]
        qh = [q[SUB * i:SUB * (i + 1)] * jnp.exp(bb[SUB * i:SUB * (i + 1)] - rs[i]) for i in range(NSUB)]
        kh = jnp.concatenate(
            [key[SUB * j:SUB * (j + 1)] * jnp.exp(re[j] - bb[SUB * j:SUB * (j + 1)]) for j in range(NSUB)], axis=0)
        pairs = [(i, j) for i in range(NSUB) for j in range(NSUB) if (j >= i if reverse else j <= i)]
        lhs = jnp.concatenate(
            [qh[i] * jnp.exp(jnp.minimum(rs[i] - re[j], EXP_CLAMP)) for (i, j) in pairs], axis=0).astype(BF16)
        kbd = jnp.where(bd, _tile_rows(kh.astype(BF16), 4), jnp.zeros((), BF16))
        prod = _dot_nt(lhs, kbd)
        a_tiles = []
        for i in range(NSUB):
            a_i = jnp.zeros((SUB, 256), F32)
            for p, (pi, pj) in enumerate(pairs):
                if pi == i:
                    a_i = jnp.where((sl8 >> 3) == pj, prod[SUB * p:SUB * (p + 1)], a_i)
            a_tiles.append(a_i)
        a_hg = jnp.where(causal256, jnp.concatenate(a_tiles, axis=0), 0.0).astype(BF16)
        vbd = jnp.where(bd, _tile_rows(v, 4), jnp.zeros((), BF16))
        o_hg = _dot(a_hg, vbd) + _dot_nt((q * jnp.exp(bb)).astype(BF16), s_hg[...].astype(BF16))
        u = _dot_tn(v, (key * jnp.exp(bend - bb)).astype(BF16))
        s_hg[...] = s_hg[...] * jnp.exp(bend) + jnp.where(bd, u, 0.0)
        if reverse:
            y_ref[rows, 0:256] = o_hg
        else:
            o_hg = o_hg + obwd_ref[rows, 0:256]
            ms = _seg_mean(o_hg * o_hg, ones_bd)
            gate = proj_ref[rows, C_HG + 1536:C_HG + 1792]
            y_ref[rows, 0:256] = (o_hg * lax.rsqrt(ms + EPS) * hgn_ref[...] * gate).astype(y_ref.dtype)

    sssd_out_ref[0] = s_ssd[0:64, :] + s_ssd[64:128, :]
    sret_out_ref[0] = s_ret[0:64, :] + s_ret[64:128, :] + s_ret[128:192, :] + s_ret[192:256, :]
    shg_out_ref[0] = s_hg[0:64, :] + s_hg[64:128, :] + s_hg[128:192, :] + s_hg[192:256, :]


def _block_table(seq_lens, reverse):
    rows = []
    blk0 = 0
    max_blocks = max(t // TB for t in seq_lens)
    for s, t in enumerate(seq_lens):
        nb = t // TB
        for k in range(nb):
            pos = (nb - 1 - k) if reverse else k
            rows.append([blk0 + pos, s, int(k == 0), int(pos == 0), int(pos == nb - 1), pos])
        blk0 += nb
    return np.asarray(rows, np.int32), blk0, max_blocks


def _tri_matrix(reverse):
    t = np.arange(TB)
    m = (t[None, :] >= t[:, None]) if reverse else (t[None, :] <= t[:, None])
    m = m & ((t[None, :] // CHUNK) == (t[:, None] // CHUNK))
    return jnp.asarray(m, BF16)


def _retention_constants(reverse):
    offset = RET_BWD_OFFSET if reverse else 0.0
    ld = np.log1p(-(2.0 ** (-5.0 - offset - np.arange(RET_HEADS, dtype=np.float64))))
    ld = np.repeat(ld, HEAD_DIM)[None, :]
    t = np.arange(CHUNK, dtype=np.float64)[:, None]
    s = (np.arange(256) % HEAD_DIM)[None, :].astype(np.float64)
    if reverse:
        b = (CHUNK - t) * ld
        dmat = np.where(s >= t, np.exp((s - t) * ld), 0.0)
    else:
        b = (t + 1.0) * ld
        dmat = np.where(s <= t, np.exp((t - s) * ld), 0.0)
    bend = CHUNK * ld
    f = lambda a: jnp.asarray(a, F32)
    return f(dmat), f(np.exp(b)), f(np.exp(bend - b)), f(np.exp(bend))


def _rope_tables(seq_lens, use_rope):
    n_pos = max(t for t, r in zip(seq_lens, use_rope) if r) if any(use_rope) else 0
    n_freq = HEAD_DIM // 4
    tok = np.arange(n_pos)
    rowp = (tok // GRID_W).astype(np.float32)
    colp = (tok % GRID_W).astype(np.float32)
    inv_freq = (ROPE_BASE ** (-np.arange(n_freq, dtype=np.float32) / n_freq)).astype(np.float32)
    ang = np.concatenate([rowp[:, None] * inv_freq, colp[:, None] * inv_freq], axis=-1)
    cos, sin = jnp.cos(jnp.asarray(ang)), jnp.sin(jnp.asarray(ang))
    cos_h = jnp.concatenate([cos, cos], axis=-1)
    sin_h = jnp.concatenate([-sin, sin], axis=-1)
    cos_t = jnp.concatenate([jnp.tile(cos_h, (1, RET_HEADS)), jnp.ones((TB, 256), F32)], axis=0)
    sin_t = jnp.concatenate([jnp.tile(sin_h, (1, RET_HEADS)), jnp.zeros((TB, 256), F32)], axis=0)
    return cos_t, sin_t, n_pos // TB


def mixer_direction(proj, edges, seq_lens, params, states, obwd, reverse):
    n_tok = proj.shape[0]
    n_seq = len(seq_lens)
    n_state = states[0].shape[0]
    tbl, n_blocks, _ = _block_table(seq_lens, reverse)
    tbl = np.concatenate([tbl[:, :6],
                          np.maximum(tbl[:, 0:1] - 1, 0),
                          np.minimum(tbl[:, 0:1] + 1, n_blocks - 1),
                          np.minimum(tbl[:, 1:2], n_state - 1),
                          (tbl[:, 1:2] < n_state).astype(np.int32)], axis=1).astype(np.int32)
    d = 1 if reverse else 0
    retd, reteq, retek, reteend = _retention_constants(reverse)
    rep = lambda a: jnp.repeat(a.astype(F32), HEAD_DIM)[None, :]
    expand = np.zeros((128, 512), np.float32)
    for h in range(SSD_HEADS):
        expand[8 * d + h, HEAD_DIM * h:HEAD_DIM * (h + 1)] = 1.0
    small = [
        _tri_matrix(reverse), jnp.asarray(expand, BF16),
        params["conv_w"].astype(F32),
        rep(params["d_skip"]),
        params["hg_norm_g"].astype(F32)[None, :], params["ssd_norm_g"].astype(F32)[None, :],
        params["ret_norm_g"].astype(F32)[None, :],
        retd, reteq, retek, reteend,
    ]
    full = lambda a: pl.BlockSpec(a.shape, lambda i, t: (0,) * a.ndim)
    in_specs = [
        pl.BlockSpec((TB, P_COLS), lambda i, t: (t[i, 0], 0)),
        pl.BlockSpec((8, 768), lambda i, t: (t[i, 6], 0)),
        pl.BlockSpec((8, 768), lambda i, t: (t[i, 7], 0)),
        pl.BlockSpec((8, 768), lambda i, t: (t[i, 0], 0)),
    ] + [full(a) for a in small] + [
        pl.BlockSpec((1, 64, 512), lambda i, t: (t[i, 8], 0, 0)),
        pl.BlockSpec((1, 64, 256), lambda i, t: (t[i, 8], 0, 0)),
        pl.BlockSpec((1, 64, 256), lambda i, t: (t[i, 8], 0, 0)),
    ]
    args = [proj, edges, edges, edges] + small + list(states)
    if not reverse:
        in_specs.append(pl.BlockSpec((TB, 1024), lambda i, t: (t[i, 0], 0)))
        args.append(obwd)
    out_dtype = F32 if reverse else BF16
    out_shape = [
        jax.ShapeDtypeStruct((n_tok, 1024), out_dtype),
        jax.ShapeDtypeStruct((n_seq, 64, 512), F32),
        jax.ShapeDtypeStruct((n_seq, 64, 256), F32),
        jax.ShapeDtypeStruct((n_seq, 64, 256), F32),
    ]
    out_specs = [
        pl.BlockSpec((TB, 1024), lambda i, t: (t[i, 0], 0)),
        pl.BlockSpec((1, 64, 512), lambda i, t: (t[i, 1], 0, 0)),
        pl.BlockSpec((1, 64, 256), lambda i, t: (t[i, 1], 0, 0)),
        pl.BlockSpec((1, 64, 256), lambda i, t: (t[i, 1], 0, 0)),
    ]
    scratch = [
        pltpu.VMEM((TB, 1280), F32),
        pltpu.VMEM((128, 512), F32),
        pltpu.VMEM((256, 256), F32),
        pltpu.VMEM((256, 256), F32),
    ]
    return pl.pallas_call(
        functools.partial(_mixer_kernel, reverse=reverse),
        grid_spec=pltpu.PrefetchScalarGridSpec(
            num_scalar_prefetch=1, grid=(n_blocks,), in_specs=in_specs, out_specs=out_specs,
            scratch_shapes=scratch),
        out_shape=out_shape,
        compiler_params=pltpu.CompilerParams(dimension_semantics=("arbitrary",), vmem_limit_bytes=48 * 1024 * 1024),
        name="mixer_bwd" if reverse else "mixer_fwd",
    )(jnp.asarray(tbl), *args)


ADA_TN = 1024


def _ada_kernel(c_ref, w_ref, b_ref, o_ref):
    s = _silu(c_ref[...]).astype(BF16)
    o_ref[0] = _dot(s, w_ref[0].astype(BF16)) + b_ref[0]


def ada_modulation(cond, w_ada, b_ada):
    n_c = cond.shape[0]
    n_l, d, n_out = w_ada.shape
    return pl.pallas_call(
        _ada_kernel,
        grid=(n_l, n_out // ADA_TN),
        in_specs=[pl.BlockSpec((n_c, d), lambda l, j: (0, 0)),
                  pl.BlockSpec((1, d, ADA_TN), lambda l, j: (l, 0, j)),
                  pl.BlockSpec((1, 1, ADA_TN), lambda l, j: (l, 0, j))],
        out_specs=pl.BlockSpec((1, n_c, ADA_TN), lambda l, j: (l, 0, j)),
        out_shape=jax.ShapeDtypeStruct((n_l, n_c, n_out), F32),
        name="ada_mod",
    )(cond, w_ada, b_ada.reshape(n_l, 1, n_out))


TM = 256


def _rms(x):
    return x * lax.rsqrt(jnp.mean(x * x, axis=-1, keepdims=True) + EPS)


def _load_tokens(ref, n_tok):
    return jnp.concatenate([ref[pl.ds(c, n_tok, stride=8), :] for c in range(8)], axis=1)


def _store_tokens(ref, val, n_tok):
    for c in range(8):
        ref[pl.ds(c, n_tok, stride=8), :] = val[:, 128 * c:128 * (c + 1)]


def _rope(a, cos, sin):
    half_lo = (lax.broadcasted_iota(jnp.int32, a.shape, 1) & 63) < 32
    ar = jnp.where(half_lo, pltpu.roll(a, 224, axis=1), pltpu.roll(a, 32, axis=1))
    return a * cos + ar * sin


def _read_tokens(refs, tiles, n_first):
    if n_first is None:
        return _load_tokens(refs[0], TM) if tiles else refs[0][...]
    first = jnp.where(pl.program_id(0) < n_first, 1.0, 0.0).astype(F32)
    return refs[0][...] * first + refs[1][...] * (1.0 - first)


def _inproj_kernel(crow_ref, ropeb_ref, *refs, tiles, n_first):
    x = _read_tokens(refs, tiles, n_first)
    (mod_ref, g_ref, w_ref, dtb_ref, alog_ref, lb_ref, cw_ref, cb_ref, cos_ref, sin_ref, o_ref,
     e_ref) = refs[1 if n_first is None else 2:]
    r = crow_ref[pl.program_id(0)]
    sh = mod_ref[pl.ds(r, 1), 0:1024]
    sc = mod_ref[pl.ds(r, 1), 1024:2048]
    h = _rms(x) * g_ref[...] * (1.0 + sc) + sh
    raw_all = _dot(h.astype(BF16), w_ref[...])

    def raw(col, width):
        return raw_all[:, col:col + width]

    cos = cos_ref[...]
    sin = sin_ref[...]
    o_ref[:, C_RET:C_RET + 256] = _rope(raw(R_RET, 256), cos, sin)
    o_ref[:, C_RET + 256:C_RET + 512] = _rope(raw(R_RET + 256, 256) * (HEAD_DIM ** -0.5), cos, sin)
    o_ref[:, C_RET + 512:C_RET + 768] = raw(R_RET + 512, 256)
    o_ref[:, C_RET + 768:C_RET + 1024] = _silu(raw(R_RET + 768, 256))
    o_ref[:, C_HG:C_HG + 256] = _silu(raw(R_HG, 256)) * (HEAD_DIM ** -0.5)
    o_ref[:, C_HG + 256:C_HG + 512] = raw(R_HG + 256, 256)
    for dirn in range(2):
        lb = lb_ref[dirn:dirn + 1, :]
        fr = raw(R_HG + 512 + 256 * dirn, 256)
        o_ref[:, C_HG + 512 + 256 * dirn:C_HG + 768 + 256 * dirn] = jnp.log(lb + (1.0 - lb) * _sigmoid(fr))
        o_ref[:, C_HG + 1024 + 256 * dirn:C_HG + 1280 + 256 * dirn] = (1.0 - lb) * _sigmoid(-fr)
    o_ref[:, C_HG + 1536:C_HG + 1792] = _silu(raw(R_HG + 1024, 256))
    o_ref[:, C_Z:C_Z + 512] = _silu(raw(R_Z, 512))
    dt = _softplus(raw(R_DT, 128) + dtb_ref[...])
    o_ref[:, C_DT:C_DT + 128] = dt
    o_ref[:, C_LA:C_LA + 128] = -dt * jnp.exp(alog_ref[...])

    xbc = raw(R_XBC, 768)
    row = lax.broadcasted_iota(jnp.int32, (CHUNK, 768), 0)
    zero_row = jnp.zeros((1, 768), F32)
    for s in range(TM // CHUNK):
        r0 = s * CHUNK
        xb = xbc[r0:r0 + CHUNK, :]
        before = xbc[r0 - 1:r0, :] if s > 0 else zero_row
        after = xbc[r0 + CHUNK:r0 + CHUNK + 1, :] if s < TM // CHUNK - 1 else zero_row
        xm1 = jnp.where(row == 0, before, pltpu.roll(xb, 1, axis=0))
        xp1 = jnp.where(row == CHUNK - 1, after, pltpu.roll(xb, CHUNK - 1, axis=0))
        pre = cw_ref[0:1, :] * xm1 + cw_ref[1:2, :] * xb + cw_ref[2:3, :] * xp1 + cb_ref[...]
        o_ref[r0:r0 + CHUNK, C_XBC:C_XBC + 768] = _silu(pre)
        if s == 0:
            e_ref[0:1, :] = xb[0:1, :]
            e_ref[2:3, :] = pre[0:1, :]
        if s == TM // CHUNK - 1:
            e_ref[1:2, :] = xb[CHUNK - 1:CHUNK, :]
            e_ref[3:4, :] = pre[CHUNK - 1:CHUNK, :]
    e_ref[4:8, :] = jnp.zeros((4, 768), F32)


def _token_operands(x, n_prefetch):
    def spec(shape, fn):
        return pl.BlockSpec(shape, lambda i, *pre: (fn(i), 0))

    if isinstance(x, tuple):
        first, second = x
        n_first, n_second = first.shape[0] // TM, second.shape[0] // TM
        specs = [spec((TM, D_MODEL), lambda i: jnp.minimum(i, n_first - 1)),
                 spec((TM, D_MODEL), lambda i: jnp.maximum(i - n_first, 0))]
        return [first, second], specs, False, n_first, (n_first + n_second) * TM
    tiles = x.shape[1] == 128
    n_tok = x.shape[0] // 8 if tiles else x.shape[0]
    return [x], [spec((TM * 8, 128) if tiles else (TM, D_MODEL), lambda i: i)], tiles, None, n_tok


def in_projection(x8, crow, rope_blk, mod, gain, w, dt_bias, a_log, lb, conv_w, conv_b, cos_t, sin_t):
    x_ops, x_specs, tiles, n_first, n_tok = _token_operands(x8, 2)
    d = D_MODEL
    compact = lambda a: jnp.zeros((1, 128), F32).at[0, :2 * SSD_HEADS].set(a.astype(F32).reshape(-1))
    const = lambda shape: pl.BlockSpec(shape, lambda i, c, rb: (0, 0))
    return pl.pallas_call(
        functools.partial(_inproj_kernel, tiles=tiles, n_first=n_first),
        grid_spec=pltpu.PrefetchScalarGridSpec(
            num_scalar_prefetch=2, grid=(n_tok // TM,),
            in_specs=x_specs + [
                      const(mod.shape), const((1, d)), const((d, R_COLS)), const((1, 128)), const((1, 128)),
                      const((2, HG_W)), const((3, SSD_CONV_CH)), const((1, SSD_CONV_CH)),
                      pl.BlockSpec((TM, 256), lambda i, c, rb: (rb[i], 0)),
                      pl.BlockSpec((TM, 256), lambda i, c, rb: (rb[i], 0))],
            out_specs=[pl.BlockSpec((TM, P_COLS), lambda i, c, rb: (i, 0)),
                       pl.BlockSpec((8, SSD_CONV_CH), lambda i, c, rb: (i, 0))]),
        out_shape=[jax.ShapeDtypeStruct((n_tok, P_COLS), F32),
                   jax.ShapeDtypeStruct((n_tok // TM * 8, SSD_CONV_CH), F32)],
        compiler_params=pltpu.CompilerParams(dimension_semantics=("arbitrary",), vmem_limit_bytes=48 * 1024 * 1024),
        name="in_proj",
    )(crow, rope_blk, *x_ops, mod, gain[None, :], w, compact(dt_bias), compact(a_log), lb.astype(F32),
      conv_w.astype(F32), conv_b.astype(F32)[None, :], cos_t, sin_t)


def _outproj_kernel(crow_ref, *refs, tiles, n_first):
    x = _read_tokens(refs, tiles, n_first)
    y_ref, mod_ref, g_ref, wo_ref, wr_ref, x1_ref, aff_ref = refs[1 if n_first is None else 2:]
    r = crow_ref[pl.program_id(0)]
    g1 = mod_ref[pl.ds(r, 1), 2048:3072]
    sh = mod_ref[pl.ds(r, 1), 3072:4096]
    sc = mod_ref[pl.ds(r, 1), 4096:5120]
    x1 = x + g1 * _dot(y_ref[...], wo_ref[...])
    _store_tokens(x1_ref, x1, TM)
    h = _rms(x1) * g_ref[...] * (1.0 + sc) + sh
    lt = _dot_nt(wr_ref[...], h.astype(BF16))
    e = jnp.exp(lt - jnp.max(lt, axis=0, keepdims=True))
    aff_ref[...] = e / jnp.sum(e, axis=0, keepdims=True)


def out_projection(y, x8, crow, mod, gain, w_out, w_router):
    x_ops, x_specs, tiles, n_first, n_tok = _token_operands(x8, 1)
    d = D_MODEL
    wr = w_router.astype(BF16).T
    return pl.pallas_call(
        functools.partial(_outproj_kernel, tiles=tiles, n_first=n_first),
        grid_spec=pltpu.PrefetchScalarGridSpec(
            num_scalar_prefetch=1, grid=(n_tok // TM,),
            in_specs=x_specs + [
                      pl.BlockSpec((TM, d), lambda i, c: (i, 0)),
                      pl.BlockSpec(mod.shape, lambda i, c: (0, 0)),
                      pl.BlockSpec((1, d), lambda i, c: (0, 0)),
                      pl.BlockSpec((d, d), lambda i, c: (0, 0)),
                      pl.BlockSpec((N_EXPERTS, d), lambda i, c: (0, 0))],
            out_specs=[pl.BlockSpec((TM * 8, 128), lambda i, c: (i, 0)),
                       pl.BlockSpec((N_EXPERTS, TM), lambda i, c: (0, i))]),
        out_shape=[jax.ShapeDtypeStruct((n_tok * 8, 128), F32), jax.ShapeDtypeStruct((N_EXPERTS, n_tok), F32)],
        compiler_params=pltpu.CompilerParams(dimension_semantics=("arbitrary",)),
        name="out_proj",
    )(crow, *x_ops, y, mod, gain[None, :], w_out, wr)


TK_BLK = 256
TK_JB = 256


def _topk_kernel(aff_ref, tri_ref, idx_ref, gate_ref, csum_scr, sel_scr, ti_scr, tg_scr, *, n, cap, cpad, npair):
    a = aff_ref[...]

    def count_ge(t):
        return jnp.sum((a >= t).astype(jnp.int32), axis=1, keepdims=True)

    def search(it, lo):
        cand = lo | (1 << (30 - it))
        return jnp.where(count_ge(pltpu.bitcast(cand, F32)) >= cap, cand, lo)

    lo_bits = lax.fori_loop(0, 31, search, jnp.zeros((npair, 1), jnp.int32))

    def refine(it, lh):
        lo, hi = lh
        mid = 0.5 * (lo + hi)
        ok = count_ge(mid) >= cap
        return jnp.where(ok, mid, lo), jnp.where(ok, hi, mid)

    lo_v, hi_v = lax.fori_loop(0, 32, refine, (pltpu.bitcast(lo_bits, F32), pltpu.bitcast(lo_bits + 1, F32)))
    gt = a >= hi_v
    eq = (a >= lo_v) & (a < hi_v)
    need = (cap - jnp.sum(gt.astype(jnp.int32), axis=1, keepdims=True)).astype(F32)

    tri = tri_ref[...]

    def prefix(mask):
        carry = jnp.zeros((npair, 1), F32)
        parts = []
        for kb in range(n // TK_BLK):
            blk = mask[:, kb * TK_BLK:(kb + 1) * TK_BLK].astype(BF16)
            loc = _dot(blk, tri) + carry
            parts.append(loc)
            carry = loc[:, TK_BLK - 1:TK_BLK]
        return jnp.concatenate(parts, axis=1)

    sel = gt | (eq & (prefix(eq) <= need))
    csum_scr[...] = prefix(sel)
    sel_scr[...] = jnp.where(sel, a, 0.0)
    tile_j = min(TK_JB, cpad)
    lane =lax.broadcasted_iota(jnp.int32, (tile_j, max(npair, 128)), 1)

    def per_expert(e, carry):
        cs = csum_scr[pl.ds(e, 1), :]
        av = sel_scr[pl.ds(e, 1), :]

        def per_block(jb, c2):
            j0 = pl.multiple_of(jb * tile_j, tile_j)
            j = (lax.broadcasted_iota(jnp.int32, (tile_j, n), 0) + j0).astype(F32)
            idx_col = jnp.sum((cs <= j).astype(F32), axis=1, keepdims=True)
            gate_col = jnp.sum(jnp.where(cs == j + 1.0, av, 0.0), axis=1, keepdims=True)
            rows = pl.ds(j0, tile_j)
            ti_scr[rows, :] = jnp.where(lane == e, idx_col, ti_scr[rows, :])
            tg_scr[rows, :] = jnp.where(lane == e, gate_col, tg_scr[rows, :])
            return c2

        return lax.fori_loop(0, cpad // tile_j, per_block, carry)

    lax.fori_loop(0, npair, per_expert, 0, unroll=4 if cpad // tile_j == 1 else 1)
    idx_ref[...] = ti_scr[...].T[0:npair, :].astype(jnp.int32)
    gate_ref[...] = tg_scr[...].T[0:npair, :]


def expert_choice_topk(aff_t, tok0, n_rows, n, rows_per_step):
    cap = EC_CAPACITY_FACTOR * n // N_EXPERTS
    cpad = max(cap, 128)
    npair = rows_per_step * N_EXPERTS
    plane = max(npair, 128)
    pairs = aff_t[:, tok0:tok0 + n_rows * n].reshape(N_EXPERTS, n_rows, n).transpose(1, 0, 2).reshape(n_rows * N_EXPERTS, n)
    t = np.arange(TK_BLK)
    tri = jnp.asarray(t[:, None] <= t[None, :], BF16)
    idx, gate = pl.pallas_call(
        functools.partial(_topk_kernel, n=n, cap=cap, cpad=cpad, npair=npair),
        grid=(n_rows // rows_per_step,),
        in_specs=[pl.BlockSpec((npair, n), lambda r: (r, 0)),
                  pl.BlockSpec((TK_BLK, TK_BLK), lambda r: (0, 0))],
        out_specs=[pl.BlockSpec((npair, cpad), lambda r: (r, 0)),
                   pl.BlockSpec((npair, cpad), lambda r: (r, 0))],
        out_shape=[jax.ShapeDtypeStruct((n_rows * N_EXPERTS, cpad), jnp.int32),
                   jax.ShapeDtypeStruct((n_rows * N_EXPERTS, cpad), F32)],
        scratch_shapes=[pltpu.VMEM((npair, n), F32), pltpu.VMEM((npair, n), F32),
                        pltpu.VMEM((cpad, plane), F32), pltpu.VMEM((cpad, plane), F32)],
        compiler_params=pltpu.CompilerParams(dimension_semantics=("arbitrary",)),
        name=f"topk_{n}",
    )(pairs, tri)
    return (idx[:, :cap].reshape(n_rows, N_EXPERTS, cap), gate[:, :cap].reshape(n_rows, N_EXPERTS, cap))


MOE_NT = 4096
MOE_C = 512
MOE_GROUP = 8


def _moe_kernel(crow_ref, idx_ref, gate_ref, x_ref, mod_ref, g_ref, wg_ref, wu_ref, wd_ref, o_ref, xs_scr, ys_scr):
    grp = pl.program_id(0)
    e = pl.program_id(1)
    r = crow_ref[grp]
    base = (grp * N_EXPERTS + e) * MOE_C
    x8 = x_ref.at[0]
    o8 = o_ref.at[0]

    def tile(i):
        return pl.ds(pl.multiple_of(i * 8, 8), 8)

    def tok_tile(p):
        return pl.ds(pl.multiple_of(idx_ref[p], 8), 8)

    @pl.when(e == 0)
    def _():
        o_ref[...] = x_ref[...]

    def gather(j, c):
        xs_scr[tile(j), :] = x8[tok_tile(base + j), :]
        return c

    lax.fori_loop(0, MOE_C, gather, 0, unroll=16)

    sh = mod_ref[pl.ds(r, 1), 3072:4096]
    sc = mod_ref[pl.ds(r, 1), 4096:5120]
    g2 = mod_ref[pl.ds(r, 1), 5120:6144]
    h = (_rms(_load_tokens(xs_scr, MOE_C)) * g_ref[...] * (1.0 + sc) + sh).astype(BF16)
    hid = _silu(_dot(h, wg_ref[0].astype(BF16))) * _dot(h, wu_ref[0].astype(BF16))
    gate_col = jnp.broadcast_to(gate_ref[0], (128, MOE_C)).T[:, 0:1]
    _store_tokens(ys_scr, _dot(hid.astype(BF16), wd_ref[0].astype(BF16)) * g2 * gate_col, MOE_C)

    def scatter(grp_j, c):
        j0 = grp_j * MOE_GROUP
        toks = [tok_tile(base + j0 + k) for k in range(MOE_GROUP)]
        vals = [o8[toks[k], :] + ys_scr[tile(j0 + k), :] for k in range(MOE_GROUP)]
        for k in range(MOE_GROUP):
            o8[toks[k], :] = vals[k]
        return c

    lax.fori_loop(0, MOE_C // MOE_GROUP, scatter, 0)


def moe_ffn(x8, idx, gate, crow_grp, mod, gain, w_gate, w_up, w_down, layer):
    n_tok, d = x8.shape[0] // 8, D_MODEL
    n_grp = n_tok // MOE_NT
    xg = x8.reshape(n_grp, MOE_NT * 8, 128)
    out = pl.pallas_call(
        _moe_kernel,
        grid_spec=pltpu.PrefetchScalarGridSpec(
            num_scalar_prefetch=2, grid=(n_grp, N_EXPERTS),
            in_specs=[pl.BlockSpec((1, 1, MOE_C), lambda g, e, c, ix: (g * N_EXPERTS + e, 0, 0)),
                      pl.BlockSpec((1, MOE_NT * 8, 128), lambda g, e, c, ix: (g, 0, 0), pipeline_mode=pl.Buffered(1)),
                      pl.BlockSpec(mod.shape, lambda g, e, c, ix: (0, 0)),
                      pl.BlockSpec((1, d), lambda g, e, c, ix: (0, 0)),
                      pl.BlockSpec((None, 1, d, D_EXPERT), lambda g, e, c, ix: (layer, e, 0, 0)),
                      pl.BlockSpec((None, 1, d, D_EXPERT), lambda g, e, c, ix: (layer, e, 0, 0)),
                      pl.BlockSpec((None, 1, D_EXPERT, d), lambda g, e, c, ix: (layer, e, 0, 0))],
            out_specs=pl.BlockSpec((1, MOE_NT * 8, 128), lambda g, e, c, ix: (g, 0, 0), pipeline_mode=pl.Buffered(1)),
            scratch_shapes=[pltpu.VMEM((MOE_C * 8, 128), F32), pltpu.VMEM((MOE_C * 8, 128), F32)]),
        out_shape=jax.ShapeDtypeStruct((n_grp, MOE_NT * 8, 128), F32),
        compiler_params=pltpu.CompilerParams(dimension_semantics=("arbitrary", "arbitrary"),
                                             vmem_limit_bytes=56 * 1024 * 1024),
        name="moe_ffn",
    )(crow_grp, idx, gate, xg, mod, gain[None, :], w_gate, w_up, w_down)
    return out.reshape(n_tok * 8, 128)


FN_TM = 1024


def _final_kernel(x_ref, g_ref, o_ref):
    o_ref[...] = _rms(_load_tokens(x_ref, FN_TM)) * g_ref[...]


def final_norm(x8, gain, tok0, n_out):
    blk0 = tok0 // FN_TM
    return pl.pallas_call(
        _final_kernel,
        grid=(n_out // FN_TM,),
        in_specs=[pl.BlockSpec((FN_TM * 8, 128), lambda i: (blk0 + i, 0)),
                  pl.BlockSpec((1, D_MODEL), lambda i: (0, 0))],
        out_specs=pl.BlockSpec((FN_TM, D_MODEL), lambda i: (i, 0)),
        out_shape=jax.ShapeDtypeStruct((n_out, D_MODEL), F32),
        name="final_norm",
    )(x8, gain[None, :])


def _ssd_to_k(s):
    return s.transpose(0, 2, 1, 3).reshape(s.shape[0], 64, 512)


def _ssd_from_k(m):
    return m.reshape(m.shape[0], 64, 8, 64).transpose(0, 2, 1, 3)


def _ret_to_k(s):
    return s.transpose(0, 2, 1, 3).reshape(s.shape[0], 64, 256)


def _ret_from_k(m):
    return m.reshape(m.shape[0], 64, 4, 64).transpose(0, 2, 1, 3)


def _hg_to_k(s):
    return s.transpose(0, 3, 1, 2).reshape(s.shape[0], 64, 256)


def _hg_from_k(m):
    return m.reshape(m.shape[0], 64, 4, 64).transpose(0, 2, 3, 1)


def _arrange_w_in(w):
    hg = w[:, 0:1280]
    z = w[:, 1280:1792]
    xbc = w[:, 1792:2560]
    dt = jnp.pad(w[:, 2560:2576], ((0, 0), (0, 128 - 2 * SSD_HEADS)))
    ret = w[:, 2576:3600]
    return jnp.concatenate([xbc, z, dt, hg, ret], axis=1).astype(BF16)


def kernel(x_prompt, x_sample, state_hgrn, state_ssd, state_ret, c, c_ctx, norm1_g, norm2_g, final_g,
           w_ada, b_ada, w_in, w_out, hg_lb, hg_norm_g, ssd_conv_w, ssd_conv_b, ssd_dt_bias, ssd_a_log,
           ssd_d, ssd_norm_g, ret_norm_g, w_router, w_gate, w_up, w_down):
    n_ctx, t_ctx, d = x_prompt.shape
    n_smp, t_smp, _ = x_sample.shape
    n_tok_s = n_smp * t_smp
    n_tok = n_tok_s + n_ctx * t_ctx
    seq_lens = [t_smp] * n_smp + [t_ctx] * n_ctx
    use_rope = [True] * n_smp + [False] * n_ctx

    x = (x_sample.reshape(n_tok_s, d), x_prompt.reshape(n_ctx * t_ctx, d))
    cond = jnp.zeros((16, d), F32).at[0].set(c_ctx).at[1:1 + n_smp].set(c)
    crow_tm = jnp.asarray([i * TM // t_smp + 1 if i * TM < n_tok_s else 0 for i in range(n_tok // TM)], jnp.int32)
    crow_grp = jnp.asarray([g + 1 if g < n_smp else 0 for g in range(n_tok // MOE_NT)], jnp.int32)
    cos_t, sin_t, ident_blk = _rope_tables(seq_lens, use_rope)
    rope_blk = jnp.asarray([i % (t_smp // TM) if i * TM < n_tok_s else ident_blk for i in range(n_tok // TM)], jnp.int32)

    lb_p = jax.nn.softmax(hg_lb.astype(F32), axis=0)
    lb_all = jnp.cumsum(lb_p, axis=0) - lb_p[0]
    mod_all = ada_modulation(cond, w_ada, b_ada)

    rows_per_grp = MOE_NT // t_ctx
    n_grp_ctx = n_ctx // rows_per_grp
    new_states = {"hg": [], "ssd": [], "ret": []}
    for l in range(DEPTH):
        mod = mod_all[l]
        proj, edges = in_projection(x, crow_tm, rope_blk, mod, norm1_g[l], _arrange_w_in(w_in[l]), ssd_dt_bias[l],
                                    ssd_a_log[l], lb_all[l], ssd_conv_w[l], ssd_conv_b[l], cos_t, sin_t)
        params = dict(conv_w=ssd_conv_w[l], d_skip=ssd_d[l], hg_norm_g=hg_norm_g[l],
                      ssd_norm_g=ssd_norm_g[l], ret_norm_g=ret_norm_g[l])
        outs = []
        for dirn in (1, 0):
            st = (_ssd_to_k(state_ssd[:, l, dirn]), _ret_to_k(state_ret[:, l, dirn]), _hg_to_k(state_hgrn[:, l, dirn]))
            outs.append(mixer_direction(proj, edges, seq_lens, params, st, outs[0][0] if outs else None, dirn == 1))
        (_, b_ssd, b_ret, b_hg), (y, f_ssd, f_ret, f_hg) = outs
        new_states["ssd"].append(jnp.stack([_ssd_from_k(f_ssd[n_smp:]), _ssd_from_k(b_ssd[n_smp:])], axis=1))
        new_states["ret"].append(jnp.stack([_ret_from_k(f_ret[n_smp:]), _ret_from_k(b_ret[n_smp:])], axis=1))
        new_states["hg"].append(jnp.stack([_hg_from_k(f_hg[n_smp:]), _hg_from_k(b_hg[n_smp:])], axis=1))

        x1, aff_t = out_projection(y, x, crow_tm, mod, norm2_g[l], w_out[l].astype(BF16), w_router[l])
        idx_s, gate_s = expert_choice_topk(aff_t, 0, n_smp, t_smp, math.gcd(n_smp, 4))
        idx_c, gate_c = expert_choice_topk(aff_t, n_tok_s, n_ctx, t_ctx, rows_per_grp)
        cap_c = idx_c.shape[-1]
        idx_c = idx_c + (jnp.arange(n_ctx, dtype=jnp.int32) % rows_per_grp)[:, None, None] * t_ctx
        regroup = lambda a: a.reshape(n_grp_ctx, rows_per_grp, N_EXPERTS, cap_c).transpose(0, 2, 1, 3).reshape(
            n_grp_ctx, N_EXPERTS, rows_per_grp * cap_c)
        idx = jnp.concatenate([idx_s, regroup(idx_c)], axis=0).reshape(-1) * 8
        gate = jnp.concatenate([gate_s, regroup(gate_c)], axis=0).reshape(-1, 1, MOE_C)
        x = moe_ffn(x1, idx, gate, crow_grp, mod, norm2_g[l], w_gate, w_up, w_down, l)

    y_sample = final_norm(x, final_g, 0, n_tok_s).reshape(n_smp, t_smp, d)
    y_prompt = final_norm(x, final_g, n_tok_s, n_ctx * t_ctx).reshape(n_ctx, t_ctx, d)
    return (y_prompt, y_sample, jnp.stack(new_states["hg"], axis=1), jnp.stack(new_states["ssd"], axis=1),
            jnp.stack(new_states["ret"], axis=1))
```

```python
import functools
import math

import jax
import jax.numpy as jnp
import numpy as np
from jax import lax
from jax.experimental import pallas as pl
from jax.experimental.pallas import tpu as pltpu

D_MODEL = 1024
DEPTH = 2
GRID_W = 64
CHUNK = 64
EPS = 1e-6
HEAD_DIM = 64
HG_W = 256
SSD_W = 512
RET_W = 256
HG_HEADS = 4
SSD_HEADS = 8
SSD_N = 64
SSD_GROUPS = 2
SSD_CONV_CH = 768
RET_HEADS = 4
RET_BWD_OFFSET = 0.5
ROPE_BASE = 10000.0
N_EXPERTS = 16
EC_CAPACITY_FACTOR = 2
D_EXPERT = 512

R_XBC = 0
R_Z = 768
R_DT = 1280
R_HG = 1408
R_RET = 2688
R_COLS = 3712

C_XBC = 0
C_Z = 768
C_DT = 1280
C_LA = 1408
C_HG = 1536
C_RET = 3328
P_COLS = 4352

TB = 256
NCH = TB // CHUNK
SUB = 8
NSUB = CHUNK // SUB
EXP_CLAMP = 80.0

F32 = jnp.float32
BF16 = jnp.bfloat16


def _sigmoid(x):
    return 1.0 / (1.0 + jnp.exp(-x))


def _silu(x):
    return x * (0.5 + 0.5 * jnp.tanh(0.5 * x))


def _softplus(x):
    return jnp.maximum(x, 0.0) + jnp.log1p(jnp.exp(-jnp.abs(x)))


def _dot(a, b):
    return jnp.dot(a, b, preferred_element_type=F32)


def _dot_nt(a, b):
    return lax.dot_general(a, b, (((1,), (1,)), ((), ())), preferred_element_type=F32)


def _dot_tn(a, b):
    return lax.dot_general(a, b, (((0,), (0,)), ((), ())), preferred_element_type=F32)


def _split3(x):
    hi = x.astype(BF16)
    r = x - hi.astype(F32)
    mid = r.astype(BF16)
    lo = (r - mid.astype(F32)).astype(BF16)
    return hi, mid, lo


def _seg_mean(x, ones_bd):
    hi = x.astype(BF16)
    lo = (x - hi.astype(F32)).astype(BF16)
    return (_dot(hi, ones_bd) + _dot(lo, ones_bd)) * (1.0 / HEAD_DIM)


def _tile_rows(x, n):
    return jnp.concatenate([x] * n, axis=0)


def _mixer_kernel(tbl_ref, proj_ref, prev_ref, next_ref, own_ref, tri_ref, expand_ref,
                  convw_ref, dskip_ref,
                  hgn_ref, ssdn_ref, retn_ref, retd_ref, reteq_ref, retek_ref, reteend_ref,
                  sssd_in_ref, sret_in_ref, shg_in_ref, *rest, reverse):
    if reverse:
        (y_ref, sssd_out_ref, sret_out_ref, shg_out_ref, b_scr, s_ssd, s_ret, s_hg) = rest
        obwd_ref = None
    else:
        (obwd_ref, y_ref, sssd_out_ref, sret_out_ref, shg_out_ref, b_scr, s_ssd, s_ret, s_hg) = rest

    step = pl.program_id(0)
    is_first = tbl_ref[step, 2]
    keep_prev = (1 - tbl_ref[step, 3]).astype(F32)
    keep_next = (1 - tbl_ref[step, 4]).astype(F32)

    r256 = lax.broadcasted_iota(jnp.int32, (256, 256), 0)
    c256 = lax.broadcasted_iota(jnp.int32, (256, 256), 1)
    bd = (r256 >> 6) == (c256 >> 6)
    gmask_s = (lax.broadcasted_iota(jnp.int32, (128, 512), 0) >> 6) == (lax.broadcasted_iota(jnp.int32, (128, 512), 1) >> 8)

    @pl.when(is_first == 1)
    def _():
        has = tbl_ref[step, 9].astype(F32)
        s_ssd[...] = jnp.where(gmask_s, _tile_rows(sssd_in_ref[0], 2), 0.0) * has
        s_ret[...] = jnp.where(bd, _tile_rows(sret_in_ref[0], 4), 0.0) * has
        s_hg[...] = jnp.where(bd, _tile_rows(shg_in_ref[0], 4), 0.0) * has

    lf_col = C_HG + (768 if reverse else 512)
    key_col = C_HG + (1280 if reverse else 1024)

    xbc_first = _silu(own_ref[2:3, :] + convw_ref[0:1, :] * prev_ref[1:2, :] * keep_prev)
    xbc_last = _silu(own_ref[3:4, :] + convw_ref[2:3, :] * next_ref[0:1, :] * keep_next)

    def split2(a):
        hi = a.astype(BF16)
        return hi, (a - hi.astype(F32)).astype(BF16)

    la_hi, la_lo = split2(proj_ref[:, C_LA:C_LA + 128])
    lf_hi, lf_lo = split2(proj_ref[:, lf_col:lf_col + 256])
    cum = _dot(tri_ref[...], jnp.concatenate([la_hi, la_lo, lf_hi, lf_lo], axis=1))
    b_scr[:, 512:768] = cum[:, 256:512] + cum[:, 512:768]
    c_hi, c_lo = split2(cum[:, 0:128] + cum[:, 128:256])
    d_hi, d_lo = split2(proj_ref[:, C_DT:C_DT + 128])
    spread = _dot(jnp.concatenate([c_hi, c_lo, d_hi, d_lo], axis=0), expand_ref[...])
    b_scr[:, 0:512] = spread[0:TB] + spread[TB:2 * TB]
    b_scr[:, 768:1280] = spread[2 * TB:3 * TB] + spread[3 * TB:4 * TB]

    gmask_b = (lax.broadcasted_iota(jnp.int32, (512, 128), 0) >> 8) == (lax.broadcasted_iota(jnp.int32, (512, 128), 1) >> 6)
    t512 = lax.broadcasted_iota(jnp.int32, (CHUNK, 512), 0)
    s512 = lax.broadcasted_iota(jnp.int32, (CHUNK, 512), 1) & 63
    t256 = lax.broadcasted_iota(jnp.int32, (CHUNK, 256), 0)
    s256 = lax.broadcasted_iota(jnp.int32, (CHUNK, 256), 1) & 63
    if reverse:
        causal512, causal256 = s512 >= t512, s256 >= t256
    else:
        causal512, causal256 = s512 <= t512, s256 <= t256
    diag512 = s512 == t512
    sl8 = lax.broadcasted_iota(jnp.int32, (SUB, 256), 1) & 63
    row768 = lax.broadcasted_iota(jnp.int32, (CHUNK, 768), 0)
    ones_bd = bd.astype(BF16)
    end_row = 0 if reverse else CHUNK - 1

    for k in range(NCH):
        c = (NCH - 1 - k) if reverse else k
        r0 = c * CHUNK
        rows = pl.ds(r0, CHUNK)

        xbc = proj_ref[rows, C_XBC:C_XBC + 768]
        if c == 0:
            xbc = jnp.where(row768 == 0, xbc_first, xbc)
        if c == NCH - 1:
            xbc = jnp.where(row768 == CHUNK - 1, xbc_last, xbc)
        bb = b_scr[rows, 0:512]
        dtc = b_scr[rows, 768:1280]
        xs = xbc[:, 0:512]
        bm = xbc[:, 512:640].astype(BF16)
        cm = xbc[:, 640:768].astype(BF16)
        xdt = xs * dtc
        bend = b_scr[pl.ds(r0 + end_row, 1), 0:512]
        rvec = jnp.sum(jnp.where(diag512, bb, 0.0), axis=0, keepdims=True)
        dmat = jnp.where(causal512, jnp.exp(jnp.minimum(bb - rvec, 0.0)), 0.0)
        rhs_b = jnp.where(gmask_b, _tile_rows(bm, 8), jnp.zeros((), BF16))
        a_ssd = (_dot_nt(cm, rhs_b) * dmat).astype(BF16)
        xdt_b = xdt.astype(BF16)
        o_parts = []
        for g in range(SSD_GROUPS):
            vbd = jnp.where(bd, _tile_rows(xdt_b[:, g * 256:(g + 1) * 256], 4), jnp.zeros((), BF16))
            o_parts.append(_dot(a_ssd[:, g * 256:(g + 1) * 256], vbd))
        o_ssd = jnp.concatenate(o_parts, axis=1) + jnp.exp(bb) * _dot(cm, s_ssd[...].astype(BF16))
        u = _dot_tn(bm, (xdt * jnp.exp(bend - bb)).astype(BF16))
        s_ssd[...] = s_ssd[...] * jnp.exp(bend) + jnp.where(gmask_s, u, 0.0)
        if reverse:
            y_ref[rows, 256:768] = o_ssd
        else:
            y = o_ssd + obwd_ref[rows, 256:768] + xs * dskip_ref[...]
            y = y * proj_ref[rows, C_Z:C_Z + 512]
            y = y * lax.rsqrt(jnp.mean(y * y, axis=-1, keepdims=True) + EPS) * ssdn_ref[...]
            y_ref[rows, 256:768] = y.astype(y_ref.dtype)

        q = proj_ref[rows, C_RET:C_RET + 256]
        kk = proj_ref[rows, C_RET + 256:C_RET + 512]
        v = proj_ref[rows, C_RET + 512:C_RET + 768].astype(BF16)
        kbd = jnp.where(bd, _tile_rows(kk.astype(BF16), 4), jnp.zeros((), BF16))
        a_ret = (_dot_nt(q.astype(BF16), kbd) * retd_ref[...]).astype(BF16)
        vbd = jnp.where(bd, _tile_rows(v, 4), jnp.zeros((), BF16))
        o_ret = _dot(a_ret, vbd) + _dot((q * reteq_ref[...]).astype(BF16), s_ret[...].astype(BF16))
        u = _dot_tn((kk * retek_ref[...]).astype(BF16), v)
        s_ret[...] = s_ret[...] * reteend_ref[...] + jnp.where(bd, u, 0.0)
        if reverse:
            y_ref[rows, 768:1024] = o_ret
        else:
            o_ret = o_ret + obwd_ref[rows, 768:1024]
            xc = o_ret - _seg_mean(o_ret, ones_bd)
            var = _seg_mean(xc * xc, ones_bd)
            gate = proj_ref[rows, C_RET + 768:C_RET + 1024]
            y_ref[rows, 768:1024] = (xc * lax.rsqrt(var + EPS) * retn_ref[...] * gate).astype(y_ref.dtype)

        bb = b_scr[rows, 512:768]
        q = proj_ref[rows, C_HG:C_HG + 256]
        v = proj_ref[rows, C_HG + 256:C_HG + 512].astype(BF16)
        key = proj_ref[rows, key_col:key_col + 256]
        bend = b_scr[pl.ds(r0 + end_row, 1), 512:768]

        def brow(r):
            return b_scr[pl.ds(r0 + r, 1), 512:768]

        zero_row = jnp.zeros((1, 256), F32)
        if reverse:
            rs = [brow(SUB * i + SUB) if i < NSUB - 1 else zero_row for i in range(NSUB)]
            re = [brow(SUB * j) for j in range(NSUB)]
        else:
            rs = [brow(SUB * i - 1) if i > 0 else zero_row for i in range(NSUB)]
            re = [brow(SUB * j + SUB - 1) for j in range(NSUB)]
        qh = [q[SUB * i:SUB * (i + 1)] * jnp.exp(bb[SUB * i:SUB * (i + 1)] - rs[i]) for i in range(NSUB)]
        kh = jnp.concatenate(
            [key[SUB * j:SUB * (j + 1)] * jnp.exp(re[j] - bb[SUB * j:SUB * (j + 1)]) for j in range(NSUB)], axis=0)
        pairs = [(i, j) for i in range(NSUB) for j in range(NSUB) if (j >= i if reverse else j <= i)]
        lhs = jnp.concatenate(
            [qh[i] * jnp.exp(jnp.minimum(rs[i] - re[j], EXP_CLAMP)) for (i, j) in pairs], axis=0).astype(BF16)
        kbd = jnp.where(bd, _tile_rows(kh.astype(BF16), 4), jnp.zeros((), BF16))
        prod = _dot_nt(lhs, kbd)
        a_tiles = []
        for i in range(NSUB):
            a_i = jnp.zeros((SUB, 256), F32)
            for p, (pi, pj) in enumerate(pairs):
                if pi == i:
                    a_i = jnp.where((sl8 >> 3) == pj, prod[SUB * p:SUB * (p + 1)], a_i)
            a_tiles.append(a_i)
        a_hg = jnp.where(causal256, jnp.concatenate(a_tiles, axis=0), 0.0).astype(BF16)
        vbd = jnp.where(bd, _tile_rows(v, 4), jnp.zeros((), BF16))
        o_hg = _dot(a_hg, vbd) + _dot_nt((q * jnp.exp(bb)).astype(BF16), s_hg[...].astype(BF16))
        u = _dot_tn(v, (key * jnp.exp(bend - bb)).astype(BF16))
        s_hg[...] = s_hg[...] * jnp.exp(bend) + jnp.where(bd, u, 0.0)
        if reverse:
            y_ref[rows, 0:256] = o_hg
        else:
            o_hg = o_hg + obwd_ref[rows, 0:256]
            ms = _seg_mean(o_hg * o_hg, ones_bd)
            gate = proj_ref[rows, C_HG + 1536:C_HG + 1792]
            y_ref[rows, 0:256] = (o_hg * lax.rsqrt(ms + EPS) * hgn_ref[...] * gate).astype(y_ref.dtype)

    sssd_out_ref[0] = s_ssd[0:64, :] + s_ssd[64:128, :]
    sret_out_ref[0] = s_ret[0:64, :] + s_ret[64:128, :] + s_ret[128:192, :] + s_ret[192:256, :]
    shg_out_ref[0] = s_hg[0:64, :] + s_hg[64:128, :] + s_hg[128:192, :] + s_hg[192:256, :]


def _block_table(seq_lens, reverse):
    rows = []
    blk0 = 0
    max_blocks = max(t // TB for t in seq_lens)
    for s, t in enumerate(seq_lens):
        nb = t // TB
        for k in range(nb):
            pos = (nb - 1 - k) if reverse else k
            rows.append([blk0 + pos, s, int(k == 0), int(pos == 0), int(pos == nb - 1), pos])
        blk0 += nb
    return np.asarray(rows, np.int32), blk0, max_blocks


def _tri_matrix(reverse):
    t = np.arange(TB)
    m = (t[None, :] >= t[:, None]) if reverse else (t[None, :] <= t[:, None])
    m = m & ((t[None, :] // CHUNK) == (t[:, None] // CHUNK))
    return jnp.asarray(m, BF16)


def _retention_constants(reverse):
    offset = RET_BWD_OFFSET if reverse else 0.0
    ld = np.log1p(-(2.0 ** (-5.0 - offset - np.arange(RET_HEADS, dtype=np.float64))))
    ld = np.repeat(ld, HEAD_DIM)[None, :]
    t = np.arange(CHUNK, dtype=np.float64)[:, None]
    s = (np.arange(256) % HEAD_DIM)[None, :].astype(np.float64)
    if reverse:
        b = (CHUNK - t) * ld
        dmat = np.where(s >= t, np.exp((s - t) * ld), 0.0)
    else:
        b = (t + 1.0) * ld
        dmat = np.where(s <= t, np.exp((t - s) * ld), 0.0)
    bend = CHUNK * ld
    f = lambda a: jnp.asarray(a, F32)
    return f(dmat), f(np.exp(b)), f(np.exp(bend - b)), f(np.exp(bend))


def _rope_tables(seq_lens, use_rope):
    n_pos = max(t for t, r in zip(seq_lens, use_rope) if r) if any(use_rope) else 0
    n_freq = HEAD_DIM // 4
    tok = np.arange(n_pos)
    rowp = (tok // GRID_W).astype(np.float32)
    colp = (tok % GRID_W).astype(np.float32)
    inv_freq = (ROPE_BASE ** (-np.arange(n_freq, dtype=np.float32) / n_freq)).astype(np.float32)
    ang = np.concatenate([rowp[:, None] * inv_freq, colp[:, None] * inv_freq], axis=-1)
    cos, sin = jnp.cos(jnp.asarray(ang)), jnp.sin(jnp.asarray(ang))
    cos_h = jnp.concatenate([cos, cos], axis=-1)
    sin_h = jnp.concatenate([-sin, sin], axis=-1)
    cos_t = jnp.concatenate([jnp.tile(cos_h, (1, RET_HEADS)), jnp.ones((TB, 256), F32)], axis=0)
    sin_t = jnp.concatenate([jnp.tile(sin_h, (1, RET_HEADS)), jnp.zeros((TB, 256), F32)], axis=0)
    return cos_t, sin_t, n_pos // TB


def mixer_direction(proj, edges, seq_lens, params, states, obwd, reverse):
    n_tok = proj.shape[0]
    n_seq = len(seq_lens)
    n_state = states[0].shape[0]
    tbl, n_blocks, _ = _block_table(seq_lens, reverse)
    tbl = np.concatenate([tbl[:, :6],
                          np.maximum(tbl[:, 0:1] - 1, 0),
                          np.minimum(tbl[:, 0:1] + 1, n_blocks - 1),
                          np.minimum(tbl[:, 1:2], n_state - 1),
                          (tbl[:, 1:2] < n_state).astype(np.int32)], axis=1).astype(np.int32)
    d = 1 if reverse else 0
    retd, reteq, retek, reteend = _retention_constants(reverse)
    rep = lambda a: jnp.repeat(a.astype(F32), HEAD_DIM)[None, :]
    expand = np.zeros((128, 512), np.float32)
    for h in range(SSD_HEADS):
        expand[8 * d + h, HEAD_DIM * h:HEAD_DIM * (h + 1)] = 1.0
    small = [
        _tri_matrix(reverse), jnp.asarray(expand, BF16),
        params["conv_w"].astype(F32),
        rep(params["d_skip"]),
        params["hg_norm_g"].astype(F32)[None, :], params["ssd_norm_g"].astype(F32)[None, :],
        params["ret_norm_g"].astype(F32)[None, :],
        retd, reteq, retek, reteend,
    ]
    full = lambda a: pl.BlockSpec(a.shape, lambda i, t: (0,) * a.ndim)
    in_specs = [
        pl.BlockSpec((TB, P_COLS), lambda i, t: (t[i, 0], 0)),
        pl.BlockSpec((8, 768), lambda i, t: (t[i, 6], 0)),
        pl.BlockSpec((8, 768), lambda i, t: (t[i, 7], 0)),
        pl.BlockSpec((8, 768), lambda i, t: (t[i, 0], 0)),
    ] + [full(a) for a in small] + [
        pl.BlockSpec((1, 64, 512), lambda i, t: (t[i, 8], 0, 0)),
        pl.BlockSpec((1, 64, 256), lambda i, t: (t[i, 8], 0, 0)),
        pl.BlockSpec((1, 64, 256), lambda i, t: (t[i, 8], 0, 0)),
    ]
    args = [proj, edges, edges, edges] + small + list(states)
    if not reverse:
        in_specs.append(pl.BlockSpec((TB, 1024), lambda i, t: (t[i, 0], 0)))
        args.append(obwd)
    out_dtype = F32 if reverse else BF16
    out_shape = [
        jax.ShapeDtypeStruct((n_tok, 1024), out_dtype),
        jax.ShapeDtypeStruct((n_seq, 64, 512), F32),
        jax.ShapeDtypeStruct((n_seq, 64, 256), F32),
        jax.ShapeDtypeStruct((n_seq, 64, 256), F32),
    ]
    out_specs = [
        pl.BlockSpec((TB, 1024), lambda i, t: (t[i, 0], 0)),
        pl.BlockSpec((1, 64, 512), lambda i, t: (t[i, 1], 0, 0)),
        pl.BlockSpec((1, 64, 256), lambda i, t: (t[i, 1], 0, 0)),
        pl.BlockSpec((1, 64, 256), lambda i, t: (t[i, 1], 0, 0)),
    ]
    scratch = [
        pltpu.VMEM((TB, 1280), F32),
        pltpu.VMEM((128, 512), F32),
        pltpu.VMEM((256, 256), F32),
        pltpu.VMEM((256, 256), F32),
    ]
    return pl.pallas_call(
        functools.partial(_mixer_kernel, reverse=reverse),
        grid_spec=pltpu.PrefetchScalarGridSpec(
            num_scalar_prefetch=1, grid=(n_blocks,), in_specs=in_specs, out_specs=out_specs,
            scratch_shapes=scratch),
        out_shape=out_shape,
        compiler_params=pltpu.CompilerParams(dimension_semantics=("arbitrary",), vmem_limit_bytes=48 * 1024 * 1024),
        name="mixer_bwd" if reverse else "mixer_fwd",
    )(jnp.asarray(tbl), *args)


ADA_TN = 1024


def _ada_kernel(c_ref, w_ref, b_ref, o_ref):
    s = _silu(c_ref[...]).astype(BF16)
    o_ref[0] = _dot(s, w_ref[0].astype(BF16)) + b_ref[0]


def ada_modulation(cond, w_ada, b_ada):
    n_c = cond.shape[0]
    n_l, d, n_out = w_ada.shape
    return pl.pallas_call(
        _ada_kernel,
        grid=(n_l, n_out // ADA_TN),
        in_specs=[pl.BlockSpec((n_c, d), lambda l, j: (0, 0)),
                  pl.BlockSpec((1, d, ADA_TN), lambda l, j: (l, 0, j)),
                  pl.BlockSpec((1, 1, ADA_TN), lambda l, j: (l, 0, j))],
        out_specs=pl.BlockSpec((1, n_c, ADA_TN), lambda l, j: (l, 0, j)),
        out_shape=jax.ShapeDtypeStruct((n_l, n_c, n_out), F32),
        name="ada_mod",
    )(cond, w_ada, b_ada.reshape(n_l, 1, n_out))


TM = 256


def _rms(x):
    return x * lax.rsqrt(jnp.mean(x * x, axis=-1, keepdims=True) + EPS)


def _load_tokens(ref, n_tok):
    return jnp.concatenate([ref[pl.ds(c, n_tok, stride=8), :] for c in range(8)], axis=1)


def _store_tokens(ref, val, n_tok):
    for c in range(8):
        ref[pl.ds(c, n_tok, stride=8), :] = val[:, 128 * c:128 * (c + 1)]


def _rope(a, cos, sin):
    half_lo = (lax.broadcasted_iota(jnp.int32, a.shape, 1) & 63) < 32
    ar = jnp.where(half_lo, pltpu.roll(a, 224, axis=1), pltpu.roll(a, 32, axis=1))
    return a * cos + ar * sin


def _read_tokens(refs, tiles, n_first):
    if n_first is None:
        return _load_tokens(refs[0], TM) if tiles else refs[0][...]
    step = jnp.full((TM, D_MODEL), pl.program_id(0), jnp.int32)
    return jnp.where(step < n_first, refs[0][...], refs[1][...])


def _inproj_kernel(crow_ref, ropeb_ref, *refs, tiles, n_first):
    x = _read_tokens(refs, tiles, n_first)
    (mod_ref, g_ref, w_ref, dtb_ref, alog_ref, lb_ref, cw_ref, cb_ref, cos_ref, sin_ref, o_ref,
     e_ref) = refs[1 if n_first is None else 2:]
    r = crow_ref[pl.program_id(0)]
    sh = mod_ref[pl.ds(r, 1), 0:1024]
    sc = mod_ref[pl.ds(r, 1), 1024:2048]
    h = _rms(x) * g_ref[...] * (1.0 + sc) + sh
    raw_all = _dot(h.astype(BF16), w_ref[...])

    def raw(col, width):
        return raw_all[:, col:col + width]

    cos = cos_ref[...]
    sin = sin_ref[...]
    o_ref[:, C_RET:C_RET + 256] = _rope(raw(R_RET, 256), cos, sin)
    o_ref[:, C_RET + 256:C_RET + 512] = _rope(raw(R_RET + 256, 256) * (HEAD_DIM ** -0.5), cos, sin)
    o_ref[:, C_RET + 512:C_RET + 768] = raw(R_RET + 512, 256)
    o_ref[:, C_RET + 768:C_RET + 1024] = _silu(raw(R_RET + 768, 256))
    o_ref[:, C_HG:C_HG + 256] = _silu(raw(R_HG, 256)) * (HEAD_DIM ** -0.5)
    o_ref[:, C_HG + 256:C_HG + 512] = raw(R_HG + 256, 256)
    for dirn in range(2):
        lb = lb_ref[dirn:dirn + 1, :]
        fr = raw(R_HG + 512 + 256 * dirn, 256)
        o_ref[:, C_HG + 512 + 256 * dirn:C_HG + 768 + 256 * dirn] = jnp.log(lb + (1.0 - lb) * _sigmoid(fr))
        o_ref[:, C_HG + 1024 + 256 * dirn:C_HG + 1280 + 256 * dirn] = (1.0 - lb) * _sigmoid(-fr)
    o_ref[:, C_HG + 1536:C_HG + 1792] = _silu(raw(R_HG + 1024, 256))
    o_ref[:, C_Z:C_Z + 512] = _silu(raw(R_Z, 512))
    dt = _softplus(raw(R_DT, 128) + dtb_ref[...])
    o_ref[:, C_DT:C_DT + 128] = dt
    o_ref[:, C_LA:C_LA + 128] = -dt * jnp.exp(alog_ref[...])

    xbc = raw(R_XBC, 768)
    row = lax.broadcasted_iota(jnp.int32, (CHUNK, 768), 0)
    zero_row = jnp.zeros((1, 768), F32)
    for s in range(TM // CHUNK):
        r0 = s * CHUNK
        xb = xbc[r0:r0 + CHUNK, :]
        before = xbc[r0 - 1:r0, :] if s > 0 else zero_row
        after = xbc[r0 + CHUNK:r0 + CHUNK + 1, :] if s < TM // CHUNK - 1 else zero_row
        xm1 = jnp.where(row == 0, before, pltpu.roll(xb, 1, axis=0))
        xp1 = jnp.where(row == CHUNK - 1, after, pltpu.roll(xb, CHUNK - 1, axis=0))
        pre = cw_ref[0:1, :] * xm1 + cw_ref[1:2, :] * xb + cw_ref[2:3, :] * xp1 + cb_ref[...]
        o_ref[r0:r0 + CHUNK, C_XBC:C_XBC + 768] = _silu(pre)
        if s == 0:
            e_ref[0:1, :] = xb[0:1, :]
            e_ref[2:3, :] = pre[0:1, :]
        if s == TM // CHUNK - 1:
            e_ref[1:2, :] = xb[CHUNK - 1:CHUNK, :]
            e_ref[3:4, :] = pre[CHUNK - 1:CHUNK, :]
    e_ref[4:8, :] = jnp.zeros((4, 768), F32)


def _token_operands(x, n_prefetch):
    def spec(shape, fn):
        return pl.BlockSpec(shape, lambda i, *pre: (fn(i), 0))

    if isinstance(x, tuple):
        first, second = x
        n_first, n_second = first.shape[0] // TM, second.shape[0] // TM
        specs = [spec((TM, D_MODEL), lambda i: jnp.minimum(i, n_first - 1)),
                 spec((TM, D_MODEL), lambda i: jnp.maximum(i - n_first, 0))]
        return [first, second], specs, False, n_first, (n_first + n_second) * TM
    tiles = x.shape[1] == 128
    n_tok = x.shape[0] // 8 if tiles else x.shape[0]
    return [x], [spec((TM * 8, 128) if tiles else (TM, D_MODEL), lambda i: i)], tiles, None, n_tok


def in_projection(x8, crow, rope_blk, mod, gain, w, dt_bias, a_log, lb, conv_w, conv_b, cos_t, sin_t):
    x_ops, x_specs, tiles, n_first, n_tok = _token_operands(x8, 2)
    d = D_MODEL
    compact = lambda a: jnp.zeros((1, 128), F32).at[0, :2 * SSD_HEADS].set(a.astype(F32).reshape(-1))
    const = lambda shape: pl.BlockSpec(shape, lambda i, c, rb: (0, 0))
    return pl.pallas_call(
        functools.partial(_inproj_kernel, tiles=tiles, n_first=n_first),
        grid_spec=pltpu.PrefetchScalarGridSpec(
            num_scalar_prefetch=2, grid=(n_tok // TM,),
            in_specs=x_specs + [
                      const(mod.shape), const((1, d)), const((d, R_COLS)), const((1, 128)), const((1, 128)),
                      const((2, HG_W)), const((3, SSD_CONV_CH)), const((1, SSD_CONV_CH)),
                      pl.BlockSpec((TM, 256), lambda i, c, rb: (rb[i], 0)),
                      pl.BlockSpec((TM, 256), lambda i, c, rb: (rb[i], 0))],
            out_specs=[pl.BlockSpec((TM, P_COLS), lambda i, c, rb: (i, 0)),
                       pl.BlockSpec((8, SSD_CONV_CH), lambda i, c, rb: (i, 0))]),
        out_shape=[jax.ShapeDtypeStruct((n_tok, P_COLS), F32),
                   jax.ShapeDtypeStruct((n_tok // TM * 8, SSD_CONV_CH), F32)],
        compiler_params=pltpu.CompilerParams(dimension_semantics=("arbitrary",), vmem_limit_bytes=48 * 1024 * 1024),
        name="in_proj",
    )(crow, rope_blk, *x_ops, mod, gain[None, :], w, compact(dt_bias), compact(a_log), lb.astype(F32),
      conv_w.astype(F32), conv_b.astype(F32)[None, :], cos_t, sin_t)


def _outproj_kernel(crow_ref, *refs, tiles, n_first):
    x = _read_tokens(refs, tiles, n_first)
    y_ref, mod_ref, g_ref, wo_ref, wr_ref, x1_ref, aff_ref = refs[1 if n_first is None else 2:]
    r = crow_ref[pl.program_id(0)]
    g1 = mod_ref[pl.ds(r, 1), 2048:3072]
    sh = mod_ref[pl.ds(r, 1), 3072:4096]
    sc = mod_ref[pl.ds(r, 1), 4096:5120]
    x1 = x + g1 * _dot(y_ref[...], wo_ref[...])
    _store_tokens(x1_ref, x1, TM)
    h = _rms(x1) * g_ref[...] * (1.0 + sc) + sh
    lt = _dot_nt(wr_ref[...], h.astype(BF16))
    e = jnp.exp(lt - jnp.max(lt, axis=0, keepdims=True))
    aff_ref[...] = e / jnp.sum(e, axis=0, keepdims=True)


def out_projection(y, x8, crow, mod, gain, w_out, w_router):
    x_ops, x_specs, tiles, n_first, n_tok = _token_operands(x8, 1)
    d = D_MODEL
    wr = w_router.astype(BF16).T
    return pl.pallas_call(
        functools.partial(_outproj_kernel, tiles=tiles, n_first=n_first),
        grid_spec=pltpu.PrefetchScalarGridSpec(
            num_scalar_prefetch=1, grid=(n_tok // TM,),
            in_specs=x_specs + [
                      pl.BlockSpec((TM, d), lambda i, c: (i, 0)),
                      pl.BlockSpec(mod.shape, lambda i, c: (0, 0)),
                      pl.BlockSpec((1, d), lambda i, c: (0, 0)),
                      pl.BlockSpec((d, d), lambda i, c: (0, 0)),
                      pl.BlockSpec((N_EXPERTS, d), lambda i, c: (0, 0))],
            out_specs=[pl.BlockSpec((TM * 8, 128), lambda i, c: (i, 0)),
                       pl.BlockSpec((N_EXPERTS, TM), lambda i, c: (0, i))]),
        out_shape=[jax.ShapeDtypeStruct((n_tok * 8, 128), F32), jax.ShapeDtypeStruct((N_EXPERTS, n_tok), F32)],
        compiler_params=pltpu.CompilerParams(dimension_semantics=("arbitrary",)),
        name="out_proj",
    )(crow, *x_ops, y, mod, gain[None, :], w_out, wr)


TK_BLK = 256
TK_JB = 256


def _topk_kernel(aff_ref, tri_ref, idx_ref, gate_ref, csum_scr, sel_scr, ti_scr, tg_scr, *, n, cap, cpad, npair):
    a = aff_ref[...]

    def count_ge(t):
        return jnp.sum((a >= t).astype(jnp.int32), axis=1, keepdims=True)

    def search(it, lo):
        cand = lo | (1 << (30 - it))
        return jnp.where(count_ge(pltpu.bitcast(cand, F32)) >= cap, cand, lo)

    lo_bits = lax.fori_loop(0, 31, search, jnp.zeros((npair, 1), jnp.int32))

    def refine(it, lh):
        lo, hi = lh
        mid = 0.5 * (lo + hi)
        ok = count_ge(mid) >= cap
        return jnp.where(ok, mid, lo), jnp.where(ok, hi, mid)

    lo_v, hi_v = lax.fori_loop(0, 32, refine, (pltpu.bitcast(lo_bits, F32), pltpu.bitcast(lo_bits + 1, F32)))
    gt = a >= hi_v
    eq = (a >= lo_v) & (a < hi_v)
    need = (cap - jnp.sum(gt.astype(jnp.int32), axis=1, keepdims=True)).astype(F32)

    tri = tri_ref[...]

    def prefix(mask):
        carry = jnp.zeros((npair, 1), F32)
        parts = []
        for kb in range(n // TK_BLK):
            blk = mask[:, kb * TK_BLK:(kb + 1) * TK_BLK].astype(BF16)
            loc = _dot(blk, tri) + carry
            parts.append(loc)
            carry = loc[:, TK_BLK - 1:TK_BLK]
        return jnp.concatenate(parts, axis=1)

    sel = gt | (eq & (prefix(eq) <= need))
    csum_scr[...] = prefix(sel)
    sel_scr[...] = jnp.where(sel, a, 0.0)
    tile_j = min(TK_JB, cpad)
    lane =lax.broadcasted_iota(jnp.int32, (tile_j, max(npair, 128)), 1)

    def per_expert(e, carry):
        cs = csum_scr[pl.ds(e, 1), :]
        av = sel_scr[pl.ds(e, 1), :]

        def per_block(jb, c2):
            j0 = pl.multiple_of(jb * tile_j, tile_j)
            j = (lax.broadcasted_iota(jnp.int32, (tile_j, n), 0) + j0).astype(F32)
            idx_col = jnp.sum((cs <= j).astype(F32), axis=1, keepdims=True)
            gate_col = jnp.sum(jnp.where(cs == j + 1.0, av, 0.0), axis=1, keepdims=True)
            rows = pl.ds(j0, tile_j)
            ti_scr[rows, :] = jnp.where(lane == e, idx_col, ti_scr[rows, :])
            tg_scr[rows, :] = jnp.where(lane == e, gate_col, tg_scr[rows, :])
            return c2

        return lax.fori_loop(0, cpad // tile_j, per_block, carry)

    lax.fori_loop(0, npair, per_expert, 0, unroll=4 if cpad // tile_j == 1 else 1)
    idx_ref[...] = ti_scr[...].T[0:npair, :].astype(jnp.int32)
    gate_ref[...] = tg_scr[...].T[0:npair, :]


def expert_choice_topk(aff_t, tok0, n_rows, n, rows_per_step):
    cap = EC_CAPACITY_FACTOR * n // N_EXPERTS
    cpad = max(cap, 128)
    npair = rows_per_step * N_EXPERTS
    plane = max(npair, 128)
    pairs = aff_t[:, tok0:tok0 + n_rows * n].reshape(N_EXPERTS, n_rows, n).transpose(1, 0, 2).reshape(n_rows * N_EXPERTS, n)
    t = np.arange(TK_BLK)
    tri = jnp.asarray(t[:, None] <= t[None, :], BF16)
    idx, gate = pl.pallas_call(
        functools.partial(_topk_kernel, n=n, cap=cap, cpad=cpad, npair=npair),
        grid=(n_rows // rows_per_step,),
        in_specs=[pl.BlockSpec((npair, n), lambda r: (r, 0)),
                  pl.BlockSpec((TK_BLK, TK_BLK), lambda r: (0, 0))],
        out_specs=[pl.BlockSpec((npair, cpad), lambda r: (r, 0)),
                   pl.BlockSpec((npair, cpad), lambda r: (r, 0))],
        out_shape=[jax.ShapeDtypeStruct((n_rows * N_EXPERTS, cpad), jnp.int32),
                   jax.ShapeDtypeStruct((n_rows * N_EXPERTS, cpad), F32)],
        scratch_shapes=[pltpu.VMEM((npair, n), F32), pltpu.VMEM((npair, n), F32),
                        pltpu.VMEM((cpad, plane), F32), pltpu.VMEM((cpad, plane), F32)],
        compiler_params=pltpu.CompilerParams(dimension_semantics=("arbitrary",)),
        name=f"topk_{n}",
    )(pairs, tri)
    return (idx[:, :cap].reshape(n_rows, N_EXPERTS, cap), gate[:, :cap].reshape(n_rows, N_EXPERTS, cap))


MOE_NT = 4096
MOE_C = 512
MOE_GROUP = 8


def _moe_kernel(crow_ref, idx_ref, gate_ref, x_ref, mod_ref, g_ref, wg_ref, wu_ref, wd_ref, o_ref, xs_scr, ys_scr):
    grp = pl.program_id(0)
    e = pl.program_id(1)
    r = crow_ref[grp]
    base = (grp * N_EXPERTS + e) * MOE_C
    x8 = x_ref.at[0]
    o8 = o_ref.at[0]

    def tile(i):
        return pl.ds(pl.multiple_of(i * 8, 8), 8)

    def tok_tile(p):
        return pl.ds(pl.multiple_of(idx_ref[p], 8), 8)

    @pl.when(e == 0)
    def _():
        o_ref[...] = x_ref[...]

    def gather(j, c):
        xs_scr[tile(j), :] = x8[tok_tile(base + j), :]
        return c

    lax.fori_loop(0, MOE_C, gather, 0, unroll=16)

    sh = mod_ref[pl.ds(r, 1), 3072:4096]
    sc = mod_ref[pl.ds(r, 1), 4096:5120]
    g2 = mod_ref[pl.ds(r, 1), 5120:6144]
    h = (_rms(_load_tokens(xs_scr, MOE_C)) * g_ref[...] * (1.0 + sc) + sh).astype(BF16)
    hid = _silu(_dot(h, wg_ref[0].astype(BF16))) * _dot(h, wu_ref[0].astype(BF16))
    gate_col = jnp.broadcast_to(gate_ref[0], (128, MOE_C)).T[:, 0:1]
    _store_tokens(ys_scr, _dot(hid.astype(BF16), wd_ref[0].astype(BF16)) * g2 * gate_col, MOE_C)

    def scatter(grp_j, c):
        j0 = grp_j * MOE_GROUP
        toks = [tok_tile(base + j0 + k) for k in range(MOE_GROUP)]
        vals = [o8[toks[k], :] + ys_scr[tile(j0 + k), :] for k in range(MOE_GROUP)]
        for k in range(MOE_GROUP):
            o8[toks[k], :] = vals[k]
        return c

    lax.fori_loop(0, MOE_C // MOE_GROUP, scatter, 0)


def moe_ffn(x8, idx, gate, crow_grp, mod, gain, w_gate, w_up, w_down, layer):
    n_tok, d = x8.shape[0] // 8, D_MODEL
    n_grp = n_tok // MOE_NT
    xg = x8.reshape(n_grp, MOE_NT * 8, 128)
    out = pl.pallas_call(
        _moe_kernel,
        grid_spec=pltpu.PrefetchScalarGridSpec(
            num_scalar_prefetch=2, grid=(n_grp, N_EXPERTS),
            in_specs=[pl.BlockSpec((1, 1, MOE_C), lambda g, e, c, ix: (g * N_EXPERTS + e, 0, 0)),
                      pl.BlockSpec((1, MOE_NT * 8, 128), lambda g, e, c, ix: (g, 0, 0), pipeline_mode=pl.Buffered(1)),
                      pl.BlockSpec(mod.shape, lambda g, e, c, ix: (0, 0)),
                      pl.BlockSpec((1, d), lambda g, e, c, ix: (0, 0)),
                      pl.BlockSpec((None, 1, d, D_EXPERT), lambda g, e, c, ix: (layer, e, 0, 0)),
                      pl.BlockSpec((None, 1, d, D_EXPERT), lambda g, e, c, ix: (layer, e, 0, 0)),
                      pl.BlockSpec((None, 1, D_EXPERT, d), lambda g, e, c, ix: (layer, e, 0, 0))],
            out_specs=pl.BlockSpec((1, MOE_NT * 8, 128), lambda g, e, c, ix: (g, 0, 0), pipeline_mode=pl.Buffered(1)),
            scratch_shapes=[pltpu.VMEM((MOE_C * 8, 128), F32), pltpu.VMEM((MOE_C * 8, 128), F32)]),
        out_shape=jax.ShapeDtypeStruct((n_grp, MOE_NT * 8, 128), F32),
        compiler_params=pltpu.CompilerParams(dimension_semantics=("arbitrary", "arbitrary"),
                                             vmem_limit_bytes=56 * 1024 * 1024),
        name="moe_ffn",
    )(crow_grp, idx, gate, xg, mod, gain[None, :], w_gate, w_up, w_down)
    return out.reshape(n_tok * 8, 128)


FN_TM = 1024


def _final_kernel(x_ref, g_ref, o_ref):
    o_ref[...] = _rms(_load_tokens(x_ref, FN_TM)) * g_ref[...]


def final_norm(x8, gain, tok0, n_out):
    blk0 = tok0 // FN_TM
    return pl.pallas_call(
        _final_kernel,
        grid=(n_out // FN_TM,),
        in_specs=[pl.BlockSpec((FN_TM * 8, 128), lambda i: (blk0 + i, 0)),
                  pl.BlockSpec((1, D_MODEL), lambda i: (0, 0))],
        out_specs=pl.BlockSpec((FN_TM, D_MODEL), lambda i: (i, 0)),
        out_shape=jax.ShapeDtypeStruct((n_out, D_MODEL), F32),
        name="final_norm",
    )(x8, gain[None, :])


def _ssd_to_k(s):
    return s.transpose(0, 2, 1, 3).reshape(s.shape[0], 64, 512)


def _ssd_from_k(m):
    return m.reshape(m.shape[0], 64, 8, 64).transpose(0, 2, 1, 3)


def _ret_to_k(s):
    return s.transpose(0, 2, 1, 3).reshape(s.shape[0], 64, 256)


def _ret_from_k(m):
    return m.reshape(m.shape[0], 64, 4, 64).transpose(0, 2, 1, 3)


def _hg_to_k(s):
    return s.transpose(0, 3, 1, 2).reshape(s.shape[0], 64, 256)


def _hg_from_k(m):
    return m.reshape(m.shape[0], 64, 4, 64).transpose(0, 2, 3, 1)


def _arrange_w_in(w):
    hg = w[:, 0:1280]
    z = w[:, 1280:1792]
    xbc = w[:, 1792:2560]
    dt = jnp.pad(w[:, 2560:2576], ((0, 0), (0, 128 - 2 * SSD_HEADS)))
    ret = w[:, 2576:3600]
    return jnp.concatenate([xbc, z, dt, hg, ret], axis=1).astype(BF16)


def kernel(x_prompt, x_sample, state_hgrn, state_ssd, state_ret, c, c_ctx, norm1_g, norm2_g, final_g,
           w_ada, b_ada, w_in, w_out, hg_lb, hg_norm_g, ssd_conv_w, ssd_conv_b, ssd_dt_bias, ssd_a_log,
           ssd_d, ssd_norm_g, ret_norm_g, w_router, w_gate, w_up, w_down):
    n_ctx, t_ctx, d = x_prompt.shape
    n_smp, t_smp, _ = x_sample.shape
    n_tok_s = n_smp * t_smp
    n_tok = n_tok_s + n_ctx * t_ctx
    seq_lens = [t_smp] * n_smp + [t_ctx] * n_ctx
    use_rope = [True] * n_smp + [False] * n_ctx

    x = (x_sample.reshape(n_tok_s, d), x_prompt.reshape(n_ctx * t_ctx, d))
    cond = jnp.zeros((16, d), F32).at[0].set(c_ctx).at[1:1 + n_smp].set(c)
    crow_tm = jnp.asarray([i * TM // t_smp + 1 if i * TM < n_tok_s else 0 for i in range(n_tok // TM)], jnp.int32)
    crow_grp = jnp.asarray([g + 1 if g < n_smp else 0 for g in range(n_tok // MOE_NT)], jnp.int32)
    cos_t, sin_t, ident_blk = _rope_tables(seq_lens, use_rope)
    rope_blk = jnp.asarray([i % (t_smp // TM) if i * TM < n_tok_s else ident_blk for i in range(n_tok // TM)], jnp.int32)

    lb_p = jax.nn.softmax(hg_lb.astype(F32), axis=0)
    lb_all = jnp.cumsum(lb_p, axis=0) - lb_p[0]
    mod_all = ada_modulation(cond, w_ada, b_ada)

    rows_per_grp = MOE_NT // t_ctx
    n_grp_ctx = n_ctx // rows_per_grp
    new_states = {"hg": [], "ssd": [], "ret": []}
    for l in range(DEPTH):
        mod = mod_all[l]
        proj, edges = in_projection(x, crow_tm, rope_blk, mod, norm1_g[l], _arrange_w_in(w_in[l]), ssd_dt_bias[l],
                                    ssd_a_log[l], lb_all[l], ssd_conv_w[l], ssd_conv_b[l], cos_t, sin_t)
        params = dict(conv_w=ssd_conv_w[l], d_skip=ssd_d[l], hg_norm_g=hg_norm_g[l],
                      ssd_norm_g=ssd_norm_g[l], ret_norm_g=ret_norm_g[l])
        outs = []
        for dirn in (1, 0):
            st = (_ssd_to_k(state_ssd[:, l, dirn]), _ret_to_k(state_ret[:, l, dirn]), _hg_to_k(state_hgrn[:, l, dirn]))
            outs.append(mixer_direction(proj, edges, seq_lens, params, st, outs[0][0] if outs else None, dirn == 1))
        (_, b_ssd, b_ret, b_hg), (y, f_ssd, f_ret, f_hg) = outs
        new_states["ssd"].append(jnp.stack([_ssd_from_k(f_ssd[n_smp:]), _ssd_from_k(b_ssd[n_smp:])], axis=1))
        new_states["ret"].append(jnp.stack([_ret_from_k(f_ret[n_smp:]), _ret_from_k(b_ret[n_smp:])], axis=1))
        new_states["hg"].append(jnp.stack([_hg_from_k(f_hg[n_smp:]), _hg_from_k(b_hg[n_smp:])], axis=1))

        x1, aff_t = out_projection(y, x, crow_tm, mod, norm2_g[l], w_out[l].astype(BF16), w_router[l])
        idx_s, gate_s = expert_choice_topk(aff_t, 0, n_smp, t_smp, math.gcd(n_smp, 8))
        idx_c, gate_c = expert_choice_topk(aff_t, n_tok_s, n_ctx, t_ctx, rows_per_grp)
        cap_c = idx_c.shape[-1]
        idx_c = idx_c + (jnp.arange(n_ctx, dtype=jnp.int32) % rows_per_grp)[:, None, None] * t_ctx
        regroup = lambda a: a.reshape(n_grp_ctx, rows_per_grp, N_EXPERTS, cap_c).transpose(0, 2, 1, 3).reshape(
            n_grp_ctx, N_EXPERTS, rows_per_grp * cap_c)
        idx = jnp.concatenate([idx_s, regroup(idx_c)], axis=0).reshape(-1) * 8
        gate = jnp.concatenate([gate_s, regroup(gate_c)], axis=0).reshape(-1, 1, MOE_C)
        x = moe_ffn(x1, idx, gate, crow_grp, mod, norm2_g[l], w_gate, w_up, w_down, l)

    y_sample = final_norm(x, final_g, 0, n_tok_s).reshape(n_smp, t_smp, d)
    y_prompt = final_norm(x, final_g, n_tok_s, n_ctx * t_ctx).reshape(n_ctx, t_ctx, d)
    return (y_prompt, y_sample, jnp.stack(new_states["hg"], axis=1), jnp.stack(new_states["ssd"], axis=1),
            jnp.stack(new_states["ret"], axis=1))
```

```python
import functools
import math

import jax
import jax.numpy as jnp
import numpy as np
from jax import lax
from jax.experimental import pallas as pl
from jax.experimental.pallas import tpu as pltpu

D_MODEL = 1024
DEPTH = 2
GRID_W = 64
CHUNK = 64
EPS = 1e-6
HEAD_DIM = 64
HG_W = 256
SSD_W = 512
RET_W = 256
HG_HEADS = 4
SSD_HEADS = 8
SSD_N = 64
SSD_GROUPS = 2
SSD_CONV_CH = 768
RET_HEADS = 4
RET_BWD_OFFSET = 0.5
ROPE_BASE = 10000.0
N_EXPERTS = 16
EC_CAPACITY_FACTOR = 2
D_EXPERT = 512

R_XBC = 0
R_Z = 768
R_DT = 1280
R_HG = 1408
R_RET = 2688
R_COLS = 3712

C_XBC = 0
C_Z = 768
C_DT = 1280
C_LA = 1408
C_HG = 1536
C_RET = 3328
P_COLS = 4352

TB = 256
NCH = TB // CHUNK
SUB = 8
NSUB = CHUNK // SUB
EXP_CLAMP = 80.0

F32 = jnp.float32
BF16 = jnp.bfloat16


def _sigmoid(x):
    return 1.0 / (1.0 + jnp.exp(-x))


def _silu(x):
    return x * (0.5 + 0.5 * jnp.tanh(0.5 * x))


def _softplus(x):
    return jnp.maximum(x, 0.0) + jnp.log1p(jnp.exp(-jnp.abs(x)))


def _dot(a, b):
    return jnp.dot(a, b, preferred_element_type=F32)


def _dot_nt(a, b):
    return lax.dot_general(a, b, (((1,), (1,)), ((), ())), preferred_element_type=F32)


def _dot_tn(a, b):
    return lax.dot_general(a, b, (((0,), (0,)), ((), ())), preferred_element_type=F32)


def _split3(x):
    hi = x.astype(BF16)
    r = x - hi.astype(F32)
    mid = r.astype(BF16)
    lo = (r - mid.astype(F32)).astype(BF16)
    return hi, mid, lo


def _seg_mean(x, ones_bd):
    hi = x.astype(BF16)
    lo = (x - hi.astype(F32)).astype(BF16)
    return (_dot(hi, ones_bd) + _dot(lo, ones_bd)) * (1.0 / HEAD_DIM)


def _tile_rows(x, n):
    return jnp.concatenate([x] * n, axis=0)


def _mixer_kernel(tbl_ref, proj_ref, prev_ref, next_ref, own_ref, tri_ref, expand_ref,
                  convw_ref, dskip_ref,
                  hgn_ref, ssdn_ref, retn_ref, retd_ref, reteq_ref, retek_ref, reteend_ref,
                  sssd_in_ref, sret_in_ref, shg_in_ref, *rest, reverse):
    if reverse:
        (y_ref, sssd_out_ref, sret_out_ref, shg_out_ref, b_scr, s_ssd, s_ret, s_hg) = rest
        obwd_ref = None
    else:
        (obwd_ref, y_ref, sssd_out_ref, sret_out_ref, shg_out_ref, b_scr, s_ssd, s_ret, s_hg) = rest

    step = pl.program_id(0)
    is_first = tbl_ref[step, 2]
    keep_prev = (1 - tbl_ref[step, 3]).astype(F32)
    keep_next = (1 - tbl_ref[step, 4]).astype(F32)

    r256 = lax.broadcasted_iota(jnp.int32, (256, 256), 0)
    c256 = lax.broadcasted_iota(jnp.int32, (256, 256), 1)
    bd = (r256 >> 6) == (c256 >> 6)
    gmask_s = (lax.broadcasted_iota(jnp.int32, (128, 512), 0) >> 6) == (lax.broadcasted_iota(jnp.int32, (128, 512), 1) >> 8)

    @pl.when(is_first == 1)
    def _():
        has = tbl_ref[step, 9].astype(F32)
        s_ssd[...] = jnp.where(gmask_s, _tile_rows(sssd_in_ref[0], 2), 0.0) * has
        s_ret[...] = jnp.where(bd, _tile_rows(sret_in_ref[0], 4), 0.0) * has
        s_hg[...] = jnp.where(bd, _tile_rows(shg_in_ref[0], 4), 0.0) * has

    lf_col = C_HG + (768 if reverse else 512)
    key_col = C_HG + (1280 if reverse else 1024)

    xbc_first = _silu(own_ref[2:3, :] + convw_ref[0:1, :] * prev_ref[1:2, :] * keep_prev)
    xbc_last = _silu(own_ref[3:4, :] + convw_ref[2:3, :] * next_ref[0:1, :] * keep_next)

    def split2(a):
        hi = a.astype(BF16)
        return hi, (a - hi.astype(F32)).astype(BF16)

    la_hi, la_lo = split2(proj_ref[:, C_LA:C_LA + 128])
    lf_hi, lf_lo = split2(proj_ref[:, lf_col:lf_col + 256])
    cum = _dot(tri_ref[...], jnp.concatenate([la_hi, la_lo, lf_hi, lf_lo], axis=1))
    b_scr[:, 512:768] = cum[:, 256:512] + cum[:, 512:768]
    c_hi, c_lo = split2(cum[:, 0:128] + cum[:, 128:256])
    d_hi, d_lo = split2(proj_ref[:, C_DT:C_DT + 128])
    spread = _dot(jnp.concatenate([c_hi, c_lo, d_hi, d_lo], axis=0), expand_ref[...])
    b_scr[:, 0:512] = spread[0:TB] + spread[TB:2 * TB]
    b_scr[:, 768:1280] = spread[2 * TB:3 * TB] + spread[3 * TB:4 * TB]

    gmask_b = (lax.broadcasted_iota(jnp.int32, (512, 128), 0) >> 8) == (lax.broadcasted_iota(jnp.int32, (512, 128), 1) >> 6)
    t512 = lax.broadcasted_iota(jnp.int32, (CHUNK, 512), 0)
    s512 = lax.broadcasted_iota(jnp.int32, (CHUNK, 512), 1) & 63
    t256 = lax.broadcasted_iota(jnp.int32, (CHUNK, 256), 0)
    s256 = lax.broadcasted_iota(jnp.int32, (CHUNK, 256), 1) & 63
    if reverse:
        causal512, causal256 = s512 >= t512, s256 >= t256
    else:
        causal512, causal256 = s512 <= t512, s256 <= t256
    diag512 = s512 == t512
    sl8 = lax.broadcasted_iota(jnp.int32, (SUB, 256), 1) & 63
    row768 = lax.broadcasted_iota(jnp.int32, (CHUNK, 768), 0)
    ones_bd = bd.astype(BF16)
    end_row = 0 if reverse else CHUNK - 1

    for k in range(NCH):
        c = (NCH - 1 - k) if reverse else k
        r0 = c * CHUNK
        rows = pl.ds(r0, CHUNK)

        xbc = proj_ref[rows, C_XBC:C_XBC + 768]
        if c == 0:
            xbc = jnp.where(row768 == 0, xbc_first, xbc)
        if c == NCH - 1:
            xbc = jnp.where(row768 == CHUNK - 1, xbc_last, xbc)
        bb = b_scr[rows, 0:512]
        dtc = b_scr[rows, 768:1280]
        xs = xbc[:, 0:512]
        bm = xbc[:, 512:640].astype(BF16)
        cm = xbc[:, 640:768].astype(BF16)
        xdt = xs * dtc
        bend = b_scr[pl.ds(r0 + end_row, 1), 0:512]
        rvec = jnp.sum(jnp.where(diag512, bb, 0.0), axis=0, keepdims=True)
        dmat = jnp.where(causal512, jnp.exp(jnp.minimum(bb - rvec, 0.0)), 0.0)
        rhs_b = jnp.where(gmask_b, _tile_rows(bm, 8), jnp.zeros((), BF16))
        a_ssd = (_dot_nt(cm, rhs_b) * dmat).astype(BF16)
        xdt_b = xdt.astype(BF16)
        o_parts = []
        for g in range(SSD_GROUPS):
            vbd = jnp.where(bd, _tile_rows(xdt_b[:, g * 256:(g + 1) * 256], 4), jnp.zeros((), BF16))
            o_parts.append(_dot(a_ssd[:, g * 256:(g + 1) * 256], vbd))
        o_ssd = jnp.concatenate(o_parts, axis=1) + jnp.exp(bb) * _dot(cm, s_ssd[...].astype(BF16))
        u = _dot_tn(bm, (xdt * jnp.exp(bend - bb)).astype(BF16))
        s_ssd[...] = s_ssd[...] * jnp.exp(bend) + jnp.where(gmask_s, u, 0.0)
        if reverse:
            y_ref[rows, 256:768] = o_ssd
        else:
            y = o_ssd + obwd_ref[rows, 256:768] + xs * dskip_ref[...]
            y = y * proj_ref[rows, C_Z:C_Z + 512]
            y = y * lax.rsqrt(jnp.mean(y * y, axis=-1, keepdims=True) + EPS) * ssdn_ref[...]
            y_ref[rows, 256:768] = y.astype(y_ref.dtype)

        q = proj_ref[rows, C_RET:C_RET + 256]
        kk = proj_ref[rows, C_RET + 256:C_RET + 512]
        v = proj_ref[rows, C_RET + 512:C_RET + 768].astype(BF16)
        kbd = jnp.where(bd, _tile_rows(kk.astype(BF16), 4), jnp.zeros((), BF16))
        a_ret = (_dot_nt(q.astype(BF16), kbd) * retd_ref[...]).astype(BF16)
        vbd = jnp.where(bd, _tile_rows(v, 4), jnp.zeros((), BF16))
        o_ret = _dot(a_ret, vbd) + _dot((q * reteq_ref[...]).astype(BF16), s_ret[...].astype(BF16))
        u = _dot_tn((kk * retek_ref[...]).astype(BF16), v)
        s_ret[...] = s_ret[...] * reteend_ref[...] + jnp.where(bd, u, 0.0)
        if reverse:
            y_ref[rows, 768:1024] = o_ret
        else:
            o_ret = o_ret + obwd_ref[rows, 768:1024]
            xc = o_ret - _seg_mean(o_ret, ones_bd)
            var = _seg_mean(xc * xc, ones_bd)
            gate = proj_ref[rows, C_RET + 768:C_RET + 1024]
            y_ref[rows, 768:1024] = (xc * lax.rsqrt(var + EPS) * retn_ref[...] * gate).astype(y_ref.dtype)

        bb = b_scr[rows, 512:768]
        q = proj_ref[rows, C_HG:C_HG + 256]
        v = proj_ref[rows, C_HG + 256:C_HG + 512].astype(BF16)
        key = proj_ref[rows, key_col:key_col + 256]
        bend = b_scr[pl.ds(r0 + end_row, 1), 512:768]

        def brow(r):
            return b_scr[pl.ds(r0 + r, 1), 512:768]

        zero_row = jnp.zeros((1, 256), F32)
        if reverse:
            rs = [brow(SUB * i + SUB) if i < NSUB - 1 else zero_row for i in range(NSUB)]
            re = [brow(SUB * j) for j in range(NSUB)]
        else:
            rs = [brow(SUB * i - 1) if i > 0 else zero_row for i in range(NSUB)]
            re = [brow(SUB * j + SUB - 1) for j in range(NSUB)]
        qh = [q[SUB * i:SUB * (i + 1)] * jnp.exp(bb[SUB * i:SUB * (i + 1)] - rs[i]) for i in range(NSUB)]
        kh = jnp.concatenate(
            [key[SUB * j:SUB * (j + 1)] * jnp.exp(re[j] - bb[SUB * j:SUB * (j + 1)]) for j in range(NSUB)], axis=0)
        pairs = [(i, j) for i in range(NSUB) for j in range(NSUB) if (j >= i if reverse else j <= i)]
        lhs = jnp.concatenate(
            [qh[i] * jnp.exp(jnp.minimum(rs[i] - re[j], EXP_CLAMP)) for (i, j) in pairs], axis=0).astype(BF16)
        kbd = jnp.where(bd, _tile_rows(kh.astype(BF16), 4), jnp.zeros((), BF16))
        prod = _dot_nt(lhs, kbd)
        a_tiles = []
        for i in range(NSUB):
            a_i = jnp.zeros((SUB, 256), F32)
            for p, (pi, pj) in enumerate(pairs):
                if pi == i:
                    a_i = jnp.where((sl8 >> 3) == pj, prod[SUB * p:SUB * (p + 1)], a_i)
            a_tiles.append(a_i)
        a_hg = jnp.where(causal256, jnp.concatenate(a_tiles, axis=0), 0.0).astype(BF16)
        vbd = jnp.where(bd, _tile_rows(v, 4), jnp.zeros((), BF16))
        o_hg = _dot(a_hg, vbd) + _dot_nt((q * jnp.exp(bb)).astype(BF16), s_hg[...].astype(BF16))
        u = _dot_tn(v, (key * jnp.exp(bend - bb)).astype(BF16))
        s_hg[...] = s_hg[...] * jnp.exp(bend) + jnp.where(bd, u, 0.0)
        if reverse:
            y_ref[rows, 0:256] = o_hg
        else:
            o_hg = o_hg + obwd_ref[rows, 0:256]
            ms = _seg_mean(o_hg * o_hg, ones_bd)
            gate = proj_ref[rows, C_HG + 1536:C_HG + 1792]
            y_ref[rows, 0:256] = (o_hg * lax.rsqrt(ms + EPS) * hgn_ref[...] * gate).astype(y_ref.dtype)

    sssd_out_ref[0] = s_ssd[0:64, :] + s_ssd[64:128, :]
    sret_out_ref[0] = s_ret[0:64, :] + s_ret[64:128, :] + s_ret[128:192, :] + s_ret[192:256, :]
    shg_out_ref[0] = s_hg[0:64, :] + s_hg[64:128, :] + s_hg[128:192, :] + s_hg[192:256, :]


def _block_table(seq_lens, reverse):
    rows = []
    blk0 = 0
    max_blocks = max(t // TB for t in seq_lens)
    for s, t in enumerate(seq_lens):
        nb = t // TB
        for k in range(nb):
            pos = (nb - 1 - k) if reverse else k
            rows.append([blk0 + pos, s, int(k == 0), int(pos == 0), int(pos == nb - 1), pos])
        blk0 += nb
    return np.asarray(rows, np.int32), blk0, max_blocks


def _tri_matrix(reverse):
    t = np.arange(TB)
    m = (t[None, :] >= t[:, None]) if reverse else (t[None, :] <= t[:, None])
    m = m & ((t[None, :] // CHUNK) == (t[:, None] // CHUNK))
    return jnp.asarray(m, BF16)


def _retention_constants(reverse):
    offset = RET_BWD_OFFSET if reverse else 0.0
    ld = np.log1p(-(2.0 ** (-5.0 - offset - np.arange(RET_HEADS, dtype=np.float64))))
    ld = np.repeat(ld, HEAD_DIM)[None, :]
    t = np.arange(CHUNK, dtype=np.float64)[:, None]
    s = (np.arange(256) % HEAD_DIM)[None, :].astype(np.float64)
    if reverse:
        b = (CHUNK - t) * ld
        dmat = np.where(s >= t, np.exp((s - t) * ld), 0.0)
    else:
        b = (t + 1.0) * ld
        dmat = np.where(s <= t, np.exp((t - s) * ld), 0.0)
    bend = CHUNK * ld
    f = lambda a: jnp.asarray(a, F32)
    return f(dmat), f(np.exp(b)), f(np.exp(bend - b)), f(np.exp(bend))


def _rope_tables(seq_lens, use_rope):
    n_pos = max(t for t, r in zip(seq_lens, use_rope) if r) if any(use_rope) else 0
    n_freq = HEAD_DIM // 4
    tok = np.arange(n_pos)
    rowp = (tok // GRID_W).astype(np.float32)
    colp = (tok % GRID_W).astype(np.float32)
    inv_freq = (ROPE_BASE ** (-np.arange(n_freq, dtype=np.float32) / n_freq)).astype(np.float32)
    ang = np.concatenate([rowp[:, None] * inv_freq, colp[:, None] * inv_freq], axis=-1)
    cos, sin = jnp.cos(jnp.asarray(ang)), jnp.sin(jnp.asarray(ang))
    cos_h = jnp.concatenate([cos, cos], axis=-1)
    sin_h = jnp.concatenate([-sin, sin], axis=-1)
    cos_t = jnp.concatenate([jnp.tile(cos_h, (1, RET_HEADS)), jnp.ones((TB, 256), F32)], axis=0)
    sin_t = jnp.concatenate([jnp.tile(sin_h, (1, RET_HEADS)), jnp.zeros((TB, 256), F32)], axis=0)
    return cos_t, sin_t, n_pos // TB


def mixer_direction(proj, edges, seq_lens, params, states, obwd, reverse):
    n_tok = proj.shape[0]
    n_seq = len(seq_lens)
    n_state = states[0].shape[0]
    tbl, n_blocks, _ = _block_table(seq_lens, reverse)
    tbl = np.concatenate([tbl[:, :6],
                          np.maximum(tbl[:, 0:1] - 1, 0),
                          np.minimum(tbl[:, 0:1] + 1, n_blocks - 1),
                          np.minimum(tbl[:, 1:2], n_state - 1),
                          (tbl[:, 1:2] < n_state).astype(np.int32)], axis=1).astype(np.int32)
    d = 1 if reverse else 0
    retd, reteq, retek, reteend = _retention_constants(reverse)
    rep = lambda a: jnp.repeat(a.astype(F32), HEAD_DIM)[None, :]
    expand = np.zeros((128, 512), np.float32)
    for h in range(SSD_HEADS):
        expand[8 * d + h, HEAD_DIM * h:HEAD_DIM * (h + 1)] = 1.0
    small = [
        _tri_matrix(reverse), jnp.asarray(expand, BF16),
        params["conv_w"].astype(F32),
        rep(params["d_skip"]),
        params["hg_norm_g"].astype(F32)[None, :], params["ssd_norm_g"].astype(F32)[None, :],
        params["ret_norm_g"].astype(F32)[None, :],
        retd, reteq, retek, reteend,
    ]
    full = lambda a: pl.BlockSpec(a.shape, lambda i, t: (0,) * a.ndim)
    in_specs = [
        pl.BlockSpec((TB, P_COLS), lambda i, t: (t[i, 0], 0)),
        pl.BlockSpec((8, 768), lambda i, t: (t[i, 6], 0)),
        pl.BlockSpec((8, 768), lambda i, t: (t[i, 7], 0)),
        pl.BlockSpec((8, 768), lambda i, t: (t[i, 0], 0)),
    ] + [full(a) for a in small] + [
        pl.BlockSpec((1, 64, 512), lambda i, t: (t[i, 8], 0, 0)),
        pl.BlockSpec((1, 64, 256), lambda i, t: (t[i, 8], 0, 0)),
        pl.BlockSpec((1, 64, 256), lambda i, t: (t[i, 8], 0, 0)),
    ]
    args = [proj, edges, edges, edges] + small + list(states)
    if not reverse:
        in_specs.append(pl.BlockSpec((TB, 1024), lambda i, t: (t[i, 0], 0)))
        args.append(obwd)
    out_dtype = F32 if reverse else BF16
    out_shape = [
        jax.ShapeDtypeStruct((n_tok, 1024), out_dtype),
        jax.ShapeDtypeStruct((n_seq, 64, 512), F32),
        jax.ShapeDtypeStruct((n_seq, 64, 256), F32),
        jax.ShapeDtypeStruct((n_seq, 64, 256), F32),
    ]
    out_specs = [
        pl.BlockSpec((TB, 1024), lambda i, t: (t[i, 0], 0)),
        pl.BlockSpec((1, 64, 512), lambda i, t: (t[i, 1], 0, 0)),
        pl.BlockSpec((1, 64, 256), lambda i, t: (t[i, 1], 0, 0)),
        pl.BlockSpec((1, 64, 256), lambda i, t: (t[i, 1], 0, 0)),
    ]
    scratch = [
        pltpu.VMEM((TB, 1280), F32),
        pltpu.VMEM((128, 512), F32),
        pltpu.VMEM((256, 256), F32),
        pltpu.VMEM((256, 256), F32),
    ]
    return pl.pallas_call(
        functools.partial(_mixer_kernel, reverse=reverse),
        grid_spec=pltpu.PrefetchScalarGridSpec(
            num_scalar_prefetch=1, grid=(n_blocks,), in_specs=in_specs, out_specs=out_specs,
            scratch_shapes=scratch),
        out_shape=out_shape,
        compiler_params=pltpu.CompilerParams(dimension_semantics=("arbitrary",), vmem_limit_bytes=48 * 1024 * 1024),
        name="mixer_bwd" if reverse else "mixer_fwd",
    )(jnp.asarray(tbl), *args)


ADA_TN = 1024


def _ada_kernel(c_ref, w_ref, b_ref, o_ref):
    s = _silu(c_ref[...]).astype(BF16)
    o_ref[0] = _dot(s, w_ref[0].astype(BF16)) + b_ref[0]


def ada_modulation(cond, w_ada, b_ada):
    n_c = cond.shape[0]
    n_l, d, n_out = w_ada.shape
    return pl.pallas_call(
        _ada_kernel,
        grid=(n_l, n_out // ADA_TN),
        in_specs=[pl.BlockSpec((n_c, d), lambda l, j: (0, 0)),
                  pl.BlockSpec((1, d, ADA_TN), lambda l, j: (l, 0, j)),
                  pl.BlockSpec((1, 1, ADA_TN), lambda l, j: (l, 0, j))],
        out_specs=pl.BlockSpec((1, n_c, ADA_TN), lambda l, j: (l, 0, j)),
        out_shape=jax.ShapeDtypeStruct((n_l, n_c, n_out), F32),
        name="ada_mod",
    )(cond, w_ada, b_ada.reshape(n_l, 1, n_out))


TM = 256


def _rms(x):
    return x * lax.rsqrt(jnp.mean(x * x, axis=-1, keepdims=True) + EPS)


def _load_tokens(ref, n_tok):
    return jnp.concatenate([ref[pl.ds(c, n_tok, stride=8), :] for c in range(8)], axis=1)


def _store_tokens(ref, val, n_tok):
    for c in range(8):
        ref[pl.ds(c, n_tok, stride=8), :] = val[:, 128 * c:128 * (c + 1)]


def _rope(a, cos, sin):
    half_lo = (lax.broadcasted_iota(jnp.int32, a.shape, 1) & 63) < 32
    ar = jnp.where(half_lo, pltpu.roll(a, 224, axis=1), pltpu.roll(a, 32, axis=1))
    return a * cos + ar * sin


def _read_tokens(refs, tiles, n_first):
    if n_first is None:
        return _load_tokens(refs[0], TM) if tiles else refs[0][...]
    step = jnp.full((TM, D_MODEL), pl.program_id(0), jnp.int32)
    return jnp.where(step < n_first, refs[0][...], refs[1][...])


def _inproj_kernel(crow_ref, ropeb_ref, *refs, tiles, n_first):
    x = _read_tokens(refs, tiles, n_first)
    (mod_ref, g_ref, w_ref, dtb_ref, alog_ref, lb_ref, cw_ref, cb_ref, cos_ref, sin_ref, o_ref,
     e_ref) = refs[1 if n_first is None else 2:]
    r = crow_ref[pl.program_id(0)]
    sh = mod_ref[pl.ds(r, 1), 0:1024]
    sc = mod_ref[pl.ds(r, 1), 1024:2048]
    h = _rms(x) * g_ref[...] * (1.0 + sc) + sh
    raw_all = _dot(h.astype(BF16), w_ref[...])

    def raw(col, width):
        return raw_all[:, col:col + width]

    cos = cos_ref[...]
    sin = sin_ref[...]
    o_ref[:, C_RET:C_RET + 256] = _rope(raw(R_RET, 256), cos, sin)
    o_ref[:, C_RET + 256:C_RET + 512] = _rope(raw(R_RET + 256, 256) * (HEAD_DIM ** -0.5), cos, sin)
    o_ref[:, C_RET + 512:C_RET + 768] = raw(R_RET + 512, 256)
    o_ref[:, C_RET + 768:C_RET + 1024] = _silu(raw(R_RET + 768, 256))
    o_ref[:, C_HG:C_HG + 256] = _silu(raw(R_HG, 256)) * (HEAD_DIM ** -0.5)
    o_ref[:, C_HG + 256:C_HG + 512] = raw(R_HG + 256, 256)
    for dirn in range(2):
        lb = lb_ref[dirn:dirn + 1, :]
        fr = raw(R_HG + 512 + 256 * dirn, 256)
        o_ref[:, C_HG + 512 + 256 * dirn:C_HG + 768 + 256 * dirn] = jnp.log(lb + (1.0 - lb) * _sigmoid(fr))
        o_ref[:, C_HG + 1024 + 256 * dirn:C_HG + 1280 + 256 * dirn] = (1.0 - lb) * _sigmoid(-fr)
    o_ref[:, C_HG + 1536:C_HG + 1792] = _silu(raw(R_HG + 1024, 256))
    o_ref[:, C_Z:C_Z + 512] = _silu(raw(R_Z, 512))
    dt = _softplus(raw(R_DT, 128) + dtb_ref[...])
    o_ref[:, C_DT:C_DT + 128] = dt
    o_ref[:, C_LA:C_LA + 128] = -dt * jnp.exp(alog_ref[...])

    xbc = raw(R_XBC, 768)
    row = lax.broadcasted_iota(jnp.int32, (CHUNK, 768), 0)
    zero_row = jnp.zeros((1, 768), F32)
    for s in range(TM // CHUNK):
        r0 = s * CHUNK
        xb = xbc[r0:r0 + CHUNK, :]
        before = xbc[r0 - 1:r0, :] if s > 0 else zero_row
        after = xbc[r0 + CHUNK:r0 + CHUNK + 1, :] if s < TM // CHUNK - 1 else zero_row
        xm1 = jnp.where(row == 0, before, pltpu.roll(xb, 1, axis=0))
        xp1 = jnp.where(row == CHUNK - 1, after, pltpu.roll(xb, CHUNK - 1, axis=0))
        pre = cw_ref[0:1, :] * xm1 + cw_ref[1:2, :] * xb + cw_ref[2:3, :] * xp1 + cb_ref[...]
        o_ref[r0:r0 + CHUNK, C_XBC:C_XBC + 768] = _silu(pre)
        if s == 0:
            e_ref[0:1, :] = xb[0:1, :]
            e_ref[2:3, :] = pre[0:1, :]
        if s == TM // CHUNK - 1:
            e_ref[1:2, :] = xb[CHUNK - 1:CHUNK, :]
            e_ref[3:4, :] = pre[CHUNK - 1:CHUNK, :]
    e_ref[4:8, :] = jnp.zeros((4, 768), F32)


def _token_operands(x, n_prefetch):
    def spec(shape, fn):
        return pl.BlockSpec(shape, lambda i, *pre: (fn(i), 0))

    if isinstance(x, tuple):
        first, second = x
        n_first, n_second = first.shape[0] // TM, second.shape[0] // TM
        specs = [spec((TM, D_MODEL), lambda i: jnp.minimum(i, n_first - 1)),
                 spec((TM, D_MODEL), lambda i: jnp.maximum(i - n_first, 0))]
        return [first, second], specs, False, n_first, (n_first + n_second) * TM
    tiles = x.shape[1] == 128
    n_tok = x.shape[0] // 8 if tiles else x.shape[0]
    return [x], [spec((TM * 8, 128) if tiles else (TM, D_MODEL), lambda i: i)], tiles, None, n_tok


def in_projection(x8, crow, rope_blk, mod, gain, w, dt_bias, a_log, lb, conv_w, conv_b, cos_t, sin_t):
    x_ops, x_specs, tiles, n_first, n_tok = _token_operands(x8, 2)
    d = D_MODEL
    compact = lambda a: jnp.zeros((1, 128), F32).at[0, :2 * SSD_HEADS].set(a.astype(F32).reshape(-1))
    const = lambda shape: pl.BlockSpec(shape, lambda i, c, rb: (0, 0))
    return pl.pallas_call(
        functools.partial(_inproj_kernel, tiles=tiles, n_first=n_first),
        grid_spec=pltpu.PrefetchScalarGridSpec(
            num_scalar_prefetch=2, grid=(n_tok // TM,),
            in_specs=x_specs + [
                      const(mod.shape), const((1, d)), const((d, R_COLS)), const((1, 128)), const((1, 128)),
                      const((2, HG_W)), const((3, SSD_CONV_CH)), const((1, SSD_CONV_CH)),
                      pl.BlockSpec((TM, 256), lambda i, c, rb: (rb[i], 0)),
                      pl.BlockSpec((TM, 256), lambda i, c, rb: (rb[i], 0))],
            out_specs=[pl.BlockSpec((TM, P_COLS), lambda i, c, rb: (i, 0)),
                       pl.BlockSpec((8, SSD_CONV_CH), lambda i, c, rb: (i, 0))]),
        out_shape=[jax.ShapeDtypeStruct((n_tok, P_COLS), F32),
                   jax.ShapeDtypeStruct((n_tok // TM * 8, SSD_CONV_CH), F32)],
        compiler_params=pltpu.CompilerParams(dimension_semantics=("arbitrary",), vmem_limit_bytes=48 * 1024 * 1024),
        name="in_proj",
    )(crow, rope_blk, *x_ops, mod, gain[None, :], w, compact(dt_bias), compact(a_log), lb.astype(F32),
      conv_w.astype(F32), conv_b.astype(F32)[None, :], cos_t, sin_t)


def _outproj_kernel(crow_ref, *refs, tiles, n_first):
    x = _read_tokens(refs, tiles, n_first)
    y_ref, mod_ref, g_ref, wo_ref, wr_ref, x1_ref, aff_ref = refs[1 if n_first is None else 2:]
    r = crow_ref[pl.program_id(0)]
    g1 = mod_ref[pl.ds(r, 1), 2048:3072]
    sh = mod_ref[pl.ds(r, 1), 3072:4096]
    sc = mod_ref[pl.ds(r, 1), 4096:5120]
    x1 = x + g1 * _dot(y_ref[...], wo_ref[...])
    _store_tokens(x1_ref, x1, TM)
    h = _rms(x1) * g_ref[...] * (1.0 + sc) + sh
    lt = _dot_nt(wr_ref[...], h.astype(BF16))
    e = jnp.exp(lt - jnp.max(lt, axis=0, keepdims=True))
    aff_ref[...] = e / jnp.sum(e, axis=0, keepdims=True)


def out_projection(y, x8, crow, mod, gain, w_out, w_router):
    x_ops, x_specs, tiles, n_first, n_tok = _token_operands(x8, 1)
    d = D_MODEL
    wr = w_router.astype(BF16).T
    return pl.pallas_call(
        functools.partial(_outproj_kernel, tiles=tiles, n_first=n_first),
        grid_spec=pltpu.PrefetchScalarGridSpec(
            num_scalar_prefetch=1, grid=(n_tok // TM,),
            in_specs=x_specs + [
                      pl.BlockSpec((TM, d), lambda i, c: (i, 0)),
                      pl.BlockSpec(mod.shape, lambda i, c: (0, 0)),
                      pl.BlockSpec((1, d), lambda i, c: (0, 0)),
                      pl.BlockSpec((d, d), lambda i, c: (0, 0)),
                      pl.BlockSpec((N_EXPERTS, d), lambda i, c: (0, 0))],
            out_specs=[pl.BlockSpec((TM * 8, 128), lambda i, c: (i, 0)),
                       pl.BlockSpec((N_EXPERTS, TM), lambda i, c: (0, i))]),
        out_shape=[jax.ShapeDtypeStruct((n_tok * 8, 128), F32), jax.ShapeDtypeStruct((N_EXPERTS, n_tok), F32)],
        compiler_params=pltpu.CompilerParams(dimension_semantics=("arbitrary",)),
        name="out_proj",
    )(crow, *x_ops, y, mod, gain[None, :], w_out, wr)


TK_BLK = 256
TK_JB = 512


def _topk_kernel(aff_ref, tri_ref, idx_ref, gate_ref, csum_scr, sel_scr, ti_scr, tg_scr, *, n, cap, cpad, npair):
    a = aff_ref[...]

    def count_ge(t):
        return jnp.sum((a >= t).astype(jnp.int32), axis=1, keepdims=True)

    def search(it, lo):
        cand = lo | (1 << (30 - it))
        return jnp.where(count_ge(pltpu.bitcast(cand, F32)) >= cap, cand, lo)

    lo_bits = lax.fori_loop(0, 31, search, jnp.zeros((npair, 1), jnp.int32))

    def refine(it, lh):
        lo, hi = lh
        mid = 0.5 * (lo + hi)
        ok = count_ge(mid) >= cap
        return jnp.where(ok, mid, lo), jnp.where(ok, hi, mid)

    lo_v, hi_v = lax.fori_loop(0, 32, refine, (pltpu.bitcast(lo_bits, F32), pltpu.bitcast(lo_bits + 1, F32)))
    gt = a >= hi_v
    eq = (a >= lo_v) & (a < hi_v)
    need = (cap - jnp.sum(gt.astype(jnp.int32), axis=1, keepdims=True)).astype(F32)

    tri = tri_ref[...]

    def prefix(mask):
        carry = jnp.zeros((npair, 1), F32)
        parts = []
        for kb in range(n // TK_BLK):
            blk = mask[:, kb * TK_BLK:(kb + 1) * TK_BLK].astype(BF16)
            loc = _dot(blk, tri) + carry
            parts.append(loc)
            carry = loc[:, TK_BLK - 1:TK_BLK]
        return jnp.concatenate(parts, axis=1)

    sel = gt | (eq & (prefix(eq) <= need))
    csum_scr[...] = prefix(sel)
    sel_scr[...] = jnp.where(sel, a, 0.0)
    tile_j = min(TK_JB, cpad)
    lane =lax.broadcasted_iota(jnp.int32, (tile_j, max(npair, 128)), 1)

    def per_expert(e, carry):
        cs = csum_scr[pl.ds(e, 1), :]
        av = sel_scr[pl.ds(e, 1), :]

        def per_block(jb, c2):
            j0 = pl.multiple_of(jb * tile_j, tile_j)
            j = (lax.broadcasted_iota(jnp.int32, (tile_j, n), 0) + j0).astype(F32)
            idx_col = jnp.sum((cs <= j).astype(F32), axis=1, keepdims=True)
            gate_col = jnp.sum(jnp.where(cs == j + 1.0, av, 0.0), axis=1, keepdims=True)
            rows = pl.ds(j0, tile_j)
            ti_scr[rows, :] = jnp.where(lane == e, idx_col, ti_scr[rows, :])
            tg_scr[rows, :] = jnp.where(lane == e, gate_col, tg_scr[rows, :])
            return c2

        return lax.fori_loop(0, cpad // tile_j, per_block, carry)

    lax.fori_loop(0, npair, per_expert, 0, unroll=4 if cpad // tile_j == 1 else 1)
    idx_ref[...] = ti_scr[...].T[0:npair, :].astype(jnp.int32)
    gate_ref[...] = tg_scr[...].T[0:npair, :]


def expert_choice_topk(aff_t, tok0, n_rows, n, rows_per_step):
    cap = EC_CAPACITY_FACTOR * n // N_EXPERTS
    cpad = max(cap, 128)
    npair = rows_per_step * N_EXPERTS
    plane = max(npair, 128)
    pairs = aff_t[:, tok0:tok0 + n_rows * n].reshape(N_EXPERTS, n_rows, n).transpose(1, 0, 2).reshape(n_rows * N_EXPERTS, n)
    t = np.arange(TK_BLK)
    tri = jnp.asarray(t[:, None] <= t[None, :], BF16)
    idx, gate = pl.pallas_call(
        functools.partial(_topk_kernel, n=n, cap=cap, cpad=cpad, npair=npair),
        grid=(n_rows // rows_per_step,),
        in_specs=[pl.BlockSpec((npair, n), lambda r: (r, 0)),
                  pl.BlockSpec((TK_BLK, TK_BLK), lambda r: (0, 0))],
        out_specs=[pl.BlockSpec((npair, cpad), lambda r: (r, 0)),
                   pl.BlockSpec((npair, cpad), lambda r: (r, 0))],
        out_shape=[jax.ShapeDtypeStruct((n_rows * N_EXPERTS, cpad), jnp.int32),
                   jax.ShapeDtypeStruct((n_rows * N_EXPERTS, cpad), F32)],
        scratch_shapes=[pltpu.VMEM((npair, n), F32), pltpu.VMEM((npair, n), F32),
                        pltpu.VMEM((cpad, plane), F32), pltpu.VMEM((cpad, plane), F32)],
        compiler_params=pltpu.CompilerParams(dimension_semantics=("arbitrary",)),
        name=f"topk_{n}",
    )(pairs, tri)
    return (idx[:, :cap].reshape(n_rows, N_EXPERTS, cap), gate[:, :cap].reshape(n_rows, N_EXPERTS, cap))


MOE_NT = 4096
MOE_C = 512
MOE_GROUP = 8


def _moe_kernel(crow_ref, idx_ref, gate_ref, x_ref, mod_ref, g_ref, wg_ref, wu_ref, wd_ref, o_ref, xs_scr, ys_scr):
    grp = pl.program_id(0)
    e = pl.program_id(1)
    r = crow_ref[grp]
    base = (grp * N_EXPERTS + e) * MOE_C
    x8 = x_ref.at[0]
    o8 = o_ref.at[0]

    def tile(i):
        return pl.ds(pl.multiple_of(i * 8, 8), 8)

    def tok_tile(p):
        return pl.ds(pl.multiple_of(idx_ref[p], 8), 8)

    @pl.when(e == 0)
    def _():
        o_ref[...] = x_ref[...]

    def gather(j, c):
        xs_scr[tile(j), :] = x8[tok_tile(base + j), :]
        return c

    lax.fori_loop(0, MOE_C, gather, 0, unroll=16)

    sh = mod_ref[pl.ds(r, 1), 3072:4096]
    sc = mod_ref[pl.ds(r, 1), 4096:5120]
    g2 = mod_ref[pl.ds(r, 1), 5120:6144]
    h = (_rms(_load_tokens(xs_scr, MOE_C)) * g_ref[...] * (1.0 + sc) + sh).astype(BF16)
    hid = _silu(_dot(h, wg_ref[0].astype(BF16))) * _dot(h, wu_ref[0].astype(BF16))
    gate_col = jnp.broadcast_to(gate_ref[0], (128, MOE_C)).T[:, 0:1]
    _store_tokens(ys_scr, _dot(hid.astype(BF16), wd_ref[0].astype(BF16)) * g2 * gate_col, MOE_C)

    def scatter(grp_j, c):
        j0 = grp_j * MOE_GROUP
        toks = [tok_tile(base + j0 + k) for k in range(MOE_GROUP)]
        vals = [o8[toks[k], :] + ys_scr[tile(j0 + k), :] for k in range(MOE_GROUP)]
        for k in range(MOE_GROUP):
            o8[toks[k], :] = vals[k]
        return c

    lax.fori_loop(0, MOE_C // MOE_GROUP, scatter, 0)


def moe_ffn(x8, idx, gate, crow_grp, mod, gain, w_gate, w_up, w_down, layer):
    n_tok, d = x8.shape[0] // 8, D_MODEL
    n_grp = n_tok // MOE_NT
    xg = x8.reshape(n_grp, MOE_NT * 8, 128)
    out = pl.pallas_call(
        _moe_kernel,
        grid_spec=pltpu.PrefetchScalarGridSpec(
            num_scalar_prefetch=2, grid=(n_grp, N_EXPERTS),
            in_specs=[pl.BlockSpec((1, 1, MOE_C), lambda g, e, c, ix: (g * N_EXPERTS + e, 0, 0)),
                      pl.BlockSpec((1, MOE_NT * 8, 128), lambda g, e, c, ix: (g, 0, 0), pipeline_mode=pl.Buffered(1)),
                      pl.BlockSpec(mod.shape, lambda g, e, c, ix: (0, 0)),
                      pl.BlockSpec((1, d), lambda g, e, c, ix: (0, 0)),
                      pl.BlockSpec((None, 1, d, D_EXPERT), lambda g, e, c, ix: (layer, e, 0, 0)),
                      pl.BlockSpec((None, 1, d, D_EXPERT), lambda g, e, c, ix: (layer, e, 0, 0)),
                      pl.BlockSpec((None, 1, D_EXPERT, d), lambda g, e, c, ix: (layer, e, 0, 0))],
            out_specs=pl.BlockSpec((1, MOE_NT * 8, 128), lambda g, e, c, ix: (g, 0, 0), pipeline_mode=pl.Buffered(1)),
            scratch_shapes=[pltpu.VMEM((MOE_C * 8, 128), F32), pltpu.VMEM((MOE_C * 8, 128), F32)]),
        out_shape=jax.ShapeDtypeStruct((n_grp, MOE_NT * 8, 128), F32),
        compiler_params=pltpu.CompilerParams(dimension_semantics=("arbitrary", "arbitrary"),
                                             vmem_limit_bytes=56 * 1024 * 1024),
        name="moe_ffn",
    )(crow_grp, idx, gate, xg, mod, gain[None, :], w_gate, w_up, w_down)
    return out.reshape(n_tok * 8, 128)


FN_TM = 1024


def _final_kernel(x_ref, g_ref, o_ref):
    o_ref[...] = _rms(_load_tokens(x_ref, FN_TM)) * g_ref[...]


def final_norm(x8, gain, tok0, n_out):
    blk0 = tok0 // FN_TM
    return pl.pallas_call(
        _final_kernel,
        grid=(n_out // FN_TM,),
        in_specs=[pl.BlockSpec((FN_TM * 8, 128), lambda i: (blk0 + i, 0)),
                  pl.BlockSpec((1, D_MODEL), lambda i: (0, 0))],
        out_specs=pl.BlockSpec((FN_TM, D_MODEL), lambda i: (i, 0)),
        out_shape=jax.ShapeDtypeStruct((n_out, D_MODEL), F32),
        name="final_norm",
    )(x8, gain[None, :])


def _ssd_to_k(s):
    return s.transpose(0, 2, 1, 3).reshape(s.shape[0], 64, 512)


def _ssd_from_k(m):
    return m.reshape(m.shape[0], 64, 8, 64).transpose(0, 2, 1, 3)


def _ret_to_k(s):
    return s.transpose(0, 2, 1, 3).reshape(s.shape[0], 64, 256)


def _ret_from_k(m):
    return m.reshape(m.shape[0], 64, 4, 64).transpose(0, 2, 1, 3)


def _hg_to_k(s):
    return s.transpose(0, 3, 1, 2).reshape(s.shape[0], 64, 256)


def _hg_from_k(m):
    return m.reshape(m.shape[0], 64, 4, 64).transpose(0, 2, 3, 1)


def _arrange_w_in(w):
    hg = w[:, 0:1280]
    z = w[:, 1280:1792]
    xbc = w[:, 1792:2560]
    dt = jnp.pad(w[:, 2560:2576], ((0, 0), (0, 128 - 2 * SSD_HEADS)))
    ret = w[:, 2576:3600]
    return jnp.concatenate([xbc, z, dt, hg, ret], axis=1).astype(BF16)


def kernel(x_prompt, x_sample, state_hgrn, state_ssd, state_ret, c, c_ctx, norm1_g, norm2_g, final_g,
           w_ada, b_ada, w_in, w_out, hg_lb, hg_norm_g, ssd_conv_w, ssd_conv_b, ssd_dt_bias, ssd_a_log,
           ssd_d, ssd_norm_g, ret_norm_g, w_router, w_gate, w_up, w_down):
    n_ctx, t_ctx, d = x_prompt.shape
    n_smp, t_smp, _ = x_sample.shape
    n_tok_s = n_smp * t_smp
    n_tok = n_tok_s + n_ctx * t_ctx
    seq_lens = [t_smp] * n_smp + [t_ctx] * n_ctx
    use_rope = [True] * n_smp + [False] * n_ctx

    x = (x_sample.reshape(n_tok_s, d), x_prompt.reshape(n_ctx * t_ctx, d))
    cond = jnp.zeros((16, d), F32).at[0].set(c_ctx).at[1:1 + n_smp].set(c)
    crow_tm = jnp.asarray([i * TM // t_smp + 1 if i * TM < n_tok_s else 0 for i in range(n_tok // TM)], jnp.int32)
    crow_grp = jnp.asarray([g + 1 if g < n_smp else 0 for g in range(n_tok // MOE_NT)], jnp.int32)
    cos_t, sin_t, ident_blk = _rope_tables(seq_lens, use_rope)
    rope_blk = jnp.asarray([i % (t_smp // TM) if i * TM < n_tok_s else ident_blk for i in range(n_tok // TM)], jnp.int32)

    lb_p = jax.nn.softmax(hg_lb.astype(F32), axis=0)
    lb_all = jnp.cumsum(lb_p, axis=0) - lb_p[0]
    mod_all = ada_modulation(cond, w_ada, b_ada)

    rows_per_grp = MOE_NT // t_ctx
    n_grp_ctx = n_ctx // rows_per_grp
    new_states = {"hg": [], "ssd": [], "ret": []}
    for l in range(DEPTH):
        mod = mod_all[l]
        proj, edges = in_projection(x, crow_tm, rope_blk, mod, norm1_g[l], _arrange_w_in(w_in[l]), ssd_dt_bias[l],
                                    ssd_a_log[l], lb_all[l], ssd_conv_w[l], ssd_conv_b[l], cos_t, sin_t)
        params = dict(conv_w=ssd_conv_w[l], d_skip=ssd_d[l], hg_norm_g=hg_norm_g[l],
                      ssd_norm_g=ssd_norm_g[l], ret_norm_g=ret_norm_g[l])
        outs = []
        for dirn in (1, 0):
            st = (_ssd_to_k(state_ssd[:, l, dirn]), _ret_to_k(state_ret[:, l, dirn]), _hg_to_k(state_hgrn[:, l, dirn]))
            outs.append(mixer_direction(proj, edges, seq_lens, params, st, outs[0][0] if outs else None, dirn == 1))
        (_, b_ssd, b_ret, b_hg), (y, f_ssd, f_ret, f_hg) = outs
        new_states["ssd"].append(jnp.stack([_ssd_from_k(f_ssd[n_smp:]), _ssd_from_k(b_ssd[n_smp:])], axis=1))
        new_states["ret"].append(jnp.stack([_ret_from_k(f_ret[n_smp:]), _ret_from_k(b_ret[n_smp:])], axis=1))
        new_states["hg"].append(jnp.stack([_hg_from_k(f_hg[n_smp:]), _hg_from_k(b_hg[n_smp:])], axis=1))

        x1, aff_t = out_projection(y, x, crow_tm, mod, norm2_g[l], w_out[l].astype(BF16), w_router[l])
        idx_s, gate_s = expert_choice_topk(aff_t, 0, n_smp, t_smp, math.gcd(n_smp, 8))
        idx_c, gate_c = expert_choice_topk(aff_t, n_tok_s, n_ctx, t_ctx, rows_per_grp)
        cap_c = idx_c.shape[-1]
        idx_c = idx_c + (jnp.arange(n_ctx, dtype=jnp.int32) % rows_per_grp)[:, None, None] * t_ctx
        regroup = lambda a: a.reshape(n_grp_ctx, rows_per_grp, N_EXPERTS, cap_c).transpose(0, 2, 1, 3).reshape(
            n_grp_ctx, N_EXPERTS, rows_per_grp * cap_c)
        idx = jnp.concatenate([idx_s, regroup(idx_c)], axis=0).reshape(-1) * 8
        gate = jnp.concatenate([gate_s, regroup(gate_c)], axis=0).reshape(-1, 1, MOE_C)
        x = moe_ffn(x1, idx, gate, crow_grp, mod, norm2_g[l], w_gate, w_up, w_down, l)

    y_sample = final_norm(x, final_g, 0, n_tok_s).reshape(n_smp, t_smp, d)
    y_prompt = final_norm(x, final_g, n_tok_s, n_ctx * t_ctx).reshape(n_ctx, t_ctx, d)
    return (y_prompt, y_sample, jnp.stack(new_states["hg"], axis=1), jnp.stack(new_states["ssd"], axis=1),
            jnp.stack(new_states["ret"], axis=1))
```
